```python
import jax, jax.numpy as jnp
from jax import lax
import numpy as np

D_MODEL = 1024
BATCH = 16
SEQ = 2048
DEPTH = 1
DEC_BATCH = 16
DEC_SEQ = 32
PAST_LEN = 2048

CHUNK = 64
N_META = 16
HEAD_DIM = 64
FOX_HEADS = 8
SB_HEADS = 8
FOX_WIDTH = FOX_HEADS * HEAD_DIM
SB_WIDTH = SB_HEADS * HEAD_DIM
Q_BLOCK = 128
NORM_EPS = 1e-6
MASK_VALUE = -1e30

PEER_HEADS = 8
PEER_N_KEYS = 128
PEER_N_EXPERTS = PEER_N_KEYS * PEER_N_KEYS
PEER_DK = 256
PEER_HALF = PEER_DK // 2
PEER_TOPK = 16
PEER_BLOCK = 256

IN_SIZES = (FOX_WIDTH, FOX_WIDTH, FOX_WIDTH, FOX_HEADS, SB_WIDTH, SB_WIDTH, SB_WIDTH, D_MODEL, D_MODEL)
IN_WIDTH = sum(IN_SIZES)
SPLIT_POINTS = tuple(int(s) for s in np.cumsum(IN_SIZES)[:-1])

kernel_name = "gated_fox_stickbreak_peer_stream_step"


def rms_norm(x, g):
    xf = x.astype(jnp.float32)
    y = xf * lax.rsqrt(jnp.mean(xf * xf, axis=-1, keepdims=True) + NORM_EPS)
    return (y * g.astype(jnp.float32)).astype(x.dtype)


def sweep_query_blocks(block_fn, q_arrays, q_pos):
    lq = q_pos.shape[0]
    blk = min(Q_BLOCK, lq)
    nb = -(-lq // blk)
    pad = nb * blk - lq

    def to_blocks(a):
        a = jnp.pad(a, [(0, 0), (0, pad)] + [(0, 0)] * (a.ndim - 2))
        a = a.reshape(a.shape[0], nb, blk, *a.shape[2:])
        return jnp.moveaxis(a, 1, 0)

    blocks = tuple(to_blocks(a) for a in q_arrays)
    pos = jnp.pad(q_pos, (0, pad), constant_values=-1).reshape(nb, blk)
    out = lax.map(lambda args: block_fn(args[0], *args[1]), (pos, blocks))
    out = jnp.moveaxis(out, 0, 1)
    return out.reshape(out.shape[0], nb * blk, *out.shape[3:])[:, :lq]


def fox_attention(q, c_q, k, v, c_k, q_pos, k_pos):
    scale = HEAD_DIM ** -0.5
    ck_t = jnp.transpose(c_k, (0, 2, 1))

    def block(qpos, qb, cqb):
        s = jnp.einsum('bqhd,bkhd->bhqk', qb, k).astype(jnp.float32) * scale
        s = s + (jnp.transpose(cqb, (0, 2, 1))[..., None] - ck_t[:, :, None, :])
        valid = k_pos[None, :] <= qpos[:, None]
        p = jax.nn.softmax(jnp.where(valid, s, MASK_VALUE), axis=-1)
        return jnp.einsum('bhqk,bkhd->bqhd', p.astype(v.dtype), v)

    return sweep_query_blocks(block, (q, c_q), q_pos)


def stick_breaking_attention(q, k, v, q_pos, k_pos):
    scale = HEAD_DIM ** -0.5

    def block(qpos, qb):
        z = jnp.einsum('bqhd,bkhd->bhqk', qb, k).astype(jnp.float32) * scale
        valid = k_pos[None, :] < qpos[:, None]
        m = jnp.where(valid, jax.nn.log_sigmoid(-z), 0.0)
        tail = lax.cumsum(m, axis=3, reverse=True) - m
        a = jnp.where(valid, jnp.exp(jax.nn.log_sigmoid(z) + tail), 0.0)
        return jnp.einsum('bhqk,bkhd->bqhd', a.astype(v.dtype), v)

    return sweep_query_blocks(block, (q,), q_pos)


def token_mixers(n, fox_k_past, fox_v_past, fox_logf_past, sb_k_past, sb_v_past,
                 w_in, b_forget, fox_q_norm, fox_k_norm, w_o_fox, w_o_sb, w_out):
    bsz, lq, _ = n.shape
    past = fox_k_past.shape[1]
    proj = n @ w_in
    fq, fk, fv, ff, sq, sk, sv, ga, gb = jnp.split(proj, SPLIT_POINTS, axis=-1)
    heads = lambda t, h: t.reshape(bsz, lq, h, HEAD_DIM)
    fq = rms_norm(heads(fq, FOX_HEADS), fox_q_norm)
    fk = rms_norm(heads(fk, FOX_HEADS), fox_k_norm)
    fv = heads(fv, FOX_HEADS)
    logf = jax.nn.log_sigmoid((ff + b_forget).astype(jnp.float32))
    sq, sk, sv = heads(sq, SB_HEADS), heads(sk, SB_HEADS), heads(sv, SB_HEADS)

    q_pos = past + jnp.arange(lq, dtype=jnp.int32)
    k_pos = jnp.arange(past + lq, dtype=jnp.int32)
    fk_all = jnp.concatenate([fox_k_past.astype(fk.dtype), fk], axis=1)
    fv_all = jnp.concatenate([fox_v_past.astype(fv.dtype), fv], axis=1)
    c = jnp.cumsum(jnp.concatenate([fox_logf_past.astype(jnp.float32), logf], axis=1), axis=1)
    sk_all = jnp.concatenate([sb_k_past.astype(sk.dtype), sk], axis=1)
    sv_all = jnp.concatenate([sb_v_past.astype(sv.dtype), sv], axis=1)

    o_fox = fox_attention(fq, c[:, past:], fk_all, fv_all, c, q_pos, k_pos)
    o_sb = stick_breaking_attention(sq, sk_all, sv_all, q_pos, k_pos)
    y_fox = o_fox.reshape(bsz, lq, FOX_WIDTH) @ w_o_fox
    y_sb = o_sb.reshape(bsz, lq, SB_WIDTH) @ w_o_sb
    merged = jax.nn.sigmoid(ga) * y_fox + jax.nn.sigmoid(gb) * y_sb
    return merged @ w_out, (fk, fv, logf, sk, sv)


def peer_ffn(n, w_peer_q, sub_keys_1, sub_keys_2, expert_u, expert_v):
    shape = n.shape
    x = n.reshape(-1, D_MODEL)
    t = x.shape[0]
    blk = min(PEER_BLOCK, t)
    nb = -(-t // blk)
    xb = jnp.pad(x, ((0, nb * blk - t), (0, 0))).reshape(nb, blk, D_MODEL)

    def block(xt):
        q = (xt @ w_peer_q).reshape(blk, PEER_HEADS, 2, PEER_HALF)
        s1 = jnp.einsum('thc,nc->thn', q[:, :, 0], sub_keys_1).astype(jnp.float32)
        s2 = jnp.einsum('thc,nc->thn', q[:, :, 1], sub_keys_2).astype(jnp.float32)
        v1, i1 = lax.top_k(s1, PEER_TOPK)
        v2, i2 = lax.top_k(s2, PEER_TOPK)
        cand = (v1[..., :, None] + v2[..., None, :]).reshape(blk, PEER_HEADS, PEER_TOPK * PEER_TOPK)
        sc, ci = lax.top_k(cand, PEER_TOPK)
        e1 = jnp.take_along_axis(i1, ci // PEER_TOPK, axis=-1)
        e2 = jnp.take_along_axis(i2, ci % PEER_TOPK, axis=-1)
        eidx = e1 * PEER_N_KEYS + e2
        g = jax.nn.softmax(sc, axis=-1)
        act = jax.nn.gelu(jnp.einsum('thkd,td->thk', expert_u[eidx], xt), approximate=False)
        w = (g * act.astype(jnp.float32)).astype(expert_v.dtype)
        return jnp.einsum('thk,thkd->td', w, expert_v[eidx])

    out = lax.map(block, xb).reshape(nb * blk, D_MODEL)[:t]
    return out.reshape(shape).astype(n.dtype)


def layer_forward(x, fox_k_past, fox_v_past, fox_logf_past, sb_k_past, sb_v_past,
                  norm_mix, w_in, b_forget, fox_q_norm, fox_k_norm, w_o_fox, w_o_sb, w_out,
                  norm_ffn, w_peer_q, sub_keys_1, sub_keys_2, expert_u, expert_v):
    y, rows = token_mixers(rms_norm(x, norm_mix), fox_k_past, fox_v_past, fox_logf_past,
                           sb_k_past, sb_v_past, w_in, b_forget, fox_q_norm, fox_k_norm,
                           w_o_fox, w_o_sb, w_out)
    h = x + y
    out = h + peer_ffn(rms_norm(h, norm_ffn), w_peer_q, sub_keys_1, sub_keys_2, expert_u, expert_v)
    return out, rows


def run_stack(x, past_of_layer, weights):
    rows = []
    for layer in range(DEPTH):
        x, r = layer_forward(x, *past_of_layer(layer), *(p[layer] for p in weights))
        rows.append(r)
    new_rows = tuple(jnp.stack(list(col), axis=0) for col in zip(*rows))
    return x, new_rows


def setup_inputs(seed: int = 0) -> dict:
    key = jax.random.key(seed)
    ks = jax.random.split(key, 24)
    nrm = lambda k, shape, s=1.0: jax.random.normal(k, shape, jnp.float32) * s
    cache_len = N_META + PAST_LEN
    return {
        "x_prompt": nrm(ks[0], (BATCH, SEQ, D_MODEL)),
        "x_sample": nrm(ks[1], (DEC_BATCH, DEC_SEQ, D_MODEL)),
        "cache_fox_k": nrm(ks[2], (DEPTH, DEC_BATCH, cache_len, FOX_HEADS, HEAD_DIM)),
        "cache_fox_v": nrm(ks[3], (DEPTH, DEC_BATCH, cache_len, FOX_HEADS, HEAD_DIM)),
        "cache_fox_logf": jax.nn.log_sigmoid(2.0 + nrm(ks[4], (DEPTH, DEC_BATCH, cache_len, FOX_HEADS), 0.5)),
        "cache_sb_k": nrm(ks[5], (DEPTH, DEC_BATCH, cache_len, SB_HEADS, HEAD_DIM)),
        "cache_sb_v": nrm(ks[6], (DEPTH, DEC_BATCH, cache_len, SB_HEADS, HEAD_DIM)),
        "meta_tokens": nrm(ks[7], (N_META, D_MODEL)),
        "norm_mix": 1.0 + nrm(ks[8], (DEPTH, D_MODEL), 0.02),
        "w_in": nrm(ks[9], (DEPTH, D_MODEL, IN_WIDTH), D_MODEL ** -0.5),
        "b_forget": 2.0 + nrm(ks[10], (DEPTH, FOX_HEADS), 0.1),
        "fox_q_norm": 1.0 + nrm(ks[11], (DEPTH, HEAD_DIM), 0.02),
        "fox_k_norm": 1.0 + nrm(ks[12], (DEPTH, HEAD_DIM), 0.02),
        "w_o_fox": nrm(ks[13], (DEPTH, FOX_WIDTH, D_MODEL), FOX_WIDTH ** -0.5),
        "w_o_sb": nrm(ks[14], (DEPTH, SB_WIDTH, D_MODEL), SB_WIDTH ** -0.5),
        "w_out": nrm(ks[15], (DEPTH, D_MODEL, D_MODEL), D_MODEL ** -0.5),
        "norm_ffn": 1.0 + nrm(ks[16], (DEPTH, D_MODEL), 0.02),
        "w_peer_q": nrm(ks[17], (DEPTH, D_MODEL, PEER_HEADS * PEER_DK), D_MODEL ** -0.5),
        "peer_sub_keys_1": nrm(ks[18], (DEPTH, PEER_N_KEYS, PEER_HALF), PEER_HALF ** -0.5),
        "peer_sub_keys_2": nrm(ks[19], (DEPTH, PEER_N_KEYS, PEER_HALF), PEER_HALF ** -0.5),
        "expert_u": nrm(ks[20], (DEPTH, PEER_N_EXPERTS, D_MODEL), D_MODEL ** -0.5),
        "expert_v": nrm(ks[21], (DEPTH, PEER_N_EXPERTS, D_MODEL), PEER_HEADS ** -0.5),
    }


def reference(x_prompt, x_sample, cache_fox_k, cache_fox_v, cache_fox_logf, cache_sb_k, cache_sb_v,
              meta_tokens, norm_mix, w_in, b_forget, fox_q_norm, fox_k_norm, w_o_fox, w_o_sb, w_out,
              norm_ffn, w_peer_q, peer_sub_keys_1, peer_sub_keys_2, expert_u, expert_v):
    weights = (norm_mix, w_in, b_forget, fox_q_norm, fox_k_norm, w_o_fox, w_o_sb, w_out,
               norm_ffn, w_peer_q, peer_sub_keys_1, peer_sub_keys_2, expert_u, expert_v)

    bsz = x_prompt.shape[0]
    dt = x_prompt.dtype
    meta = jnp.broadcast_to(meta_tokens.astype(dt)[None], (bsz, N_META, D_MODEL))
    x_full = jnp.concatenate([meta, x_prompt], axis=1)
    empty_fox = jnp.zeros((bsz, 0, FOX_HEADS, HEAD_DIM), dt)
    empty_logf = jnp.zeros((bsz, 0, FOX_HEADS), jnp.float32)
    empty_sb = jnp.zeros((bsz, 0, SB_HEADS, HEAD_DIM), dt)
    prompt_past = lambda layer: (empty_fox, empty_fox, empty_logf, empty_sb, empty_sb)
    y_full, (fk_p, fv_p, fl_p, sk_p, sv_p) = run_stack(x_full, prompt_past, weights)
    y_prompt = y_full[:, N_META:]

    sample_past = lambda layer: (cache_fox_k[layer], cache_fox_v[layer], cache_fox_logf[layer],
                                 cache_sb_k[layer], cache_sb_v[layer])
    y_sample, (fk_s, fv_s, fl_s, sk_s, sv_s) = run_stack(x_sample, sample_past, weights)

    return (y_prompt, y_sample, fk_p, fv_p, fl_p, sk_p, sv_p, fk_s, fv_s, fl_s, sk_s, sv_s)
```

```python
import functools

import jax
import jax.numpy as jnp
from jax import lax
from jax.experimental import pallas as pl
from jax.experimental.pallas import tpu as pltpu

F32 = jnp.float32
BF16 = jnp.bfloat16

D_MODEL = 1024
HEAD_DIM = 64
N_HEADS = 8
WIDTH = N_HEADS * HEAD_DIM
N_PAIRS = N_HEADS // 2
NORM_EPS = 1e-6
MASK_VALUE = -1e30
ATTN_SCALE = HEAD_DIM ** -0.5

PEER_HEADS = 8
PEER_N_KEYS = 128
PEER_HALF = 128
PEER_TOPK = 16
PEER_PICKS = PEER_HEADS * PEER_TOPK

LANES = 128
SUBLANES = 8
ROWS_PER_EXPERT = D_MODEL // 2 // LANES

TOKEN_BLOCK = 256
ATTN_BLOCK = 128
PEER_TOKEN_BLOCK = 128
VMEM_LIMIT = 56 * 1024 * 1024


def _cparams(n_grid, vmem=None):
    return pltpu.CompilerParams(
        dimension_semantics=("arbitrary",) * n_grid,
        vmem_limit_bytes=vmem,
    )


def _dot(a, b):
    return jnp.dot(a, b, preferred_element_type=F32)


def _dot_nt(a, b):
    return lax.dot_general(a, b, (((1,), (1,)), ((), ())), preferred_element_type=F32)


def _split_dot(a, b_bf16, terms):
    out = None
    rem = a
    for i in range(terms):
        part = rem.astype(BF16)
        d = _dot(part, b_bf16)
        out = d if out is None else out + d
        if i + 1 < terms:
            rem = rem - part.astype(F32)
    return out


def _log_sigmoid(x):
    return jnp.minimum(x, 0.0) - jnp.log1p(jnp.exp(-jnp.abs(x)))


def _sigmoid(x):
    return 1.0 / (1.0 + jnp.exp(-x))


def _in_proj_kernel(x_ref, g_ref, wa_ref, wf_ref, wb_ref, wg_ref, bf_ref, qn_ref, kn_ref, bd_ref,
                    fq_ref, fk_ref, fv_ref, lf_ref, sq_ref, sk_ref, sv_ref, ga_ref, gb_ref):
    x = x_ref[...]
    ms = jnp.mean(x * x, axis=-1, keepdims=True)
    n = (x * lax.rsqrt(ms + NORM_EPS) * g_ref[...]).astype(BF16)

    a = _dot(n, wa_ref[...])
    bd = bd_ref[...]

    def head_norm(t, gain):
        msq = _split_dot(t * t, bd, 2) * (1.0 / HEAD_DIM)
        return t * lax.rsqrt(msq + NORM_EPS) * gain

    fq_ref[...] = head_norm(a[:, :WIDTH], qn_ref[...])
    fk_ref[...] = head_norm(a[:, WIDTH:2 * WIDTH], kn_ref[...])
    fv_ref[...] = a[:, 2 * WIDTH:]

    f = _dot(n, wf_ref[...]) + bf_ref[...]
    lf_ref[...] = _log_sigmoid(f)[:, :N_HEADS]

    b = _dot(n, wb_ref[...])
    sq_ref[...] = b[:, :WIDTH]
    sk_ref[...] = b[:, WIDTH:2 * WIDTH]
    sv_ref[...] = b[:, 2 * WIDTH:]

    g = _dot(n, wg_ref[...])
    ga_ref[...] = _sigmoid(g[:, :D_MODEL])
    gb_ref[...] = _sigmoid(g[:, D_MODEL:])


def _in_proj(x, p):
    t = x.shape[0]
    tl = min(TOKEN_BLOCK, t)
    assert t % tl == 0
    row = lambda w: pl.BlockSpec((tl, w), lambda i: (i, 0))
    full = lambda a: pl.BlockSpec(a.shape, lambda i: (0,) * a.ndim)
    consts = (p["g_mix"], p["w_a"], p["w_f"], p["w_b"], p["w_g"], p["b_f"], p["qn"], p["kn"], p["bd"])
    widths = (WIDTH, WIDTH, WIDTH, N_HEADS, WIDTH, WIDTH, WIDTH, D_MODEL, D_MODEL)
    return pl.pallas_call(
        _in_proj_kernel,
        out_shape=tuple(jax.ShapeDtypeStruct((t, w), F32) for w in widths),
        grid=(t // tl,),
        in_specs=[row(D_MODEL)] + [full(c) for c in consts],
        out_specs=tuple(row(w) for w in widths),
        compiler_params=_cparams(1, VMEM_LIMIT),
        name="in_proj",
    )(x, *consts)


def _cumsum_kernel(x_ref, o_ref):
    n_chunks = x_ref.shape[1] // LANES
    r = lax.broadcasted_iota(jnp.int32, (LANES, LANES), 0)
    c = lax.broadcasted_iota(jnp.int32, (LANES, LANES), 1)
    upper = jnp.where(r <= c, 1.0, 0.0).astype(BF16)
    carry = jnp.zeros((x_ref.shape[0], 1), F32)
    for k in range(n_chunks):
        y = _split_dot(x_ref[:, k * LANES:(k + 1) * LANES], upper, 3) + carry
        o_ref[:, k * LANES:(k + 1) * LANES] = y
        carry = y[:, LANES - 1:LANES]


def _logf_cumsum(lf_rows):
    return pl.pallas_call(
        _cumsum_kernel,
        out_shape=jax.ShapeDtypeStruct(lf_rows.shape, F32),
        name="logf_cumsum",
    )(lf_rows)


def _head_masks():
    lane = lax.broadcasted_iota(jnp.int32, (1, LANES), 1)
    return [lane < HEAD_DIM, lane >= HEAD_DIM]


def _past_chunks(p_len):
    n_full = p_len // ATTN_BLOCK
    tail = p_len - n_full * ATTN_BLOCK
    return n_full, tail


def _fox_kernel(q_ref, kn_ref, vn_ref, kp_ref, vp_ref, cq_ref, cn_ref, cp_ref, o_ref, *, tq, p_len):
    lq = q_ref.shape[1]
    nq = lq // tq
    n_full, tail = _past_chunks(p_len)
    masks = _head_masks()
    row = lax.broadcasted_iota(jnp.int32, (tq, tq), 0)
    col = lax.broadcasted_iota(jnp.int32, (tq, tq), 1)
    causal = col <= row

    def q_block(i):
        q0 = i * tq if nq == 1 else pl.multiple_of(i * tq, tq)
        qblk = q_ref[0, pl.ds(q0, tq), :] * ATTN_SCALE
        cqb = cq_ref[0, 0, pl.ds(q0, tq), :]
        outs = []
        for hh in range(2):
            qh = jnp.where(masks[hh], qblk, 0.0).astype(BF16)
            cq = cqb[:, hh:hh + 1]

            def step(carry, kb, vb, ck, valid=None):
                m, l, acc = carry
                s = _dot_nt(qh, kb.astype(BF16)) + (cq - ck)
                if valid is not None:
                    s = jnp.where(valid, s, MASK_VALUE)
                m_new = jnp.maximum(m, jnp.max(s, axis=-1, keepdims=True))
                alpha = jnp.exp(m - m_new)
                pr = jnp.exp(s - m_new)
                l = alpha * l + jnp.sum(pr, axis=-1, keepdims=True)
                acc = alpha * acc + _dot(pr.astype(BF16), vb.astype(BF16))
                return m_new, l, acc

            carry = (jnp.full((tq, 1), MASK_VALUE, F32), jnp.zeros((tq, 1), F32),
                     jnp.zeros((tq, LANES), F32))

            def past_body(c, carry):
                k0 = pl.multiple_of(c * ATTN_BLOCK, ATTN_BLOCK)
                return step(carry, kp_ref[0, pl.ds(k0, ATTN_BLOCK), :], vp_ref[0, pl.ds(k0, ATTN_BLOCK), :],
                            cp_ref[0, 0, c][hh:hh + 1, :])

            if n_full:
                carry = lax.fori_loop(0, n_full, past_body, carry)
            if tail:
                k0 = n_full * ATTN_BLOCK
                carry = step(carry, kp_ref[0, k0:k0 + tail, :], vp_ref[0, k0:k0 + tail, :],
                             cp_ref[0, 0, n_full][hh:hh + 1, :tail])

            def new_body(j, carry):
                k0 = pl.multiple_of(j * tq, tq)
                return step(carry, kn_ref[0, pl.ds(k0, tq), :], vn_ref[0, pl.ds(k0, tq), :],
                            cn_ref[0, 0, j][hh:hh + 1, :])

            if nq > 1:
                carry = lax.fori_loop(0, i, new_body, carry)
            m, l, acc = step(carry, kn_ref[0, pl.ds(q0, tq), :], vn_ref[0, pl.ds(q0, tq), :],
                             cn_ref[0, 0, i][hh:hh + 1, :], valid=causal)
            outs.append(acc / l)
        o_ref[0, pl.ds(q0, tq), :] = jnp.where(masks[0], outs[0], outs[1])

    if nq == 1:
        q_block(0)
    else:
        lax.fori_loop(0, nq, lambda i, _: (q_block(i), 0)[1], 0)


def _sb_kernel(q_ref, kn_ref, vn_ref, kp_ref, vp_ref, o_ref, *, tq, p_len):
    lq = q_ref.shape[1]
    nq = lq // tq
    n_full, tail = _past_chunks(p_len)
    masks = _head_masks()

    def tri(n):
        r = lax.broadcasted_iota(jnp.int32, (n, n), 0)
        c = lax.broadcasted_iota(jnp.int32, (n, n), 1)
        return r, c, jnp.where(r > c, 1.0, 0.0).astype(BF16)

    row, col, tri_q = tri(tq)
    strictly_before = col < row
    tri_full = tri(ATTN_BLOCK)[2] if n_full else None
    tri_tail = tri(tail)[2] if tail else None

    def q_block(i):
        q0 = i * tq if nq == 1 else pl.multiple_of(i * tq, tq)
        qblk = q_ref[0, pl.ds(q0, tq), :] * ATTN_SCALE
        outs = []
        for hh in range(2):
            qh = jnp.where(masks[hh], qblk, 0.0).astype(BF16)

            def step(carry, kb, vb, tri_m, valid=None):
                run, acc = carry
                z = _dot_nt(qh, kb.astype(BF16))
                m = _log_sigmoid(-z)
                if valid is not None:
                    m = jnp.where(valid, m, 0.0)
                tail_in = _split_dot(m, tri_m, 2)
                a = jnp.exp(z + m + tail_in + run)
                if valid is not None:
                    a = jnp.where(valid, a, 0.0)
                acc = acc + _dot(a.astype(BF16), vb.astype(BF16))
                run = run + tail_in[:, 0:1] + m[:, 0:1]
                return run, acc

            carry = (jnp.zeros((tq, 1), F32), jnp.zeros((tq, LANES), F32))
            carry = step(carry, kn_ref[0, pl.ds(q0, tq), :], vn_ref[0, pl.ds(q0, tq), :],
                         tri_q, valid=strictly_before)

            def new_body(jj, carry):
                k0 = pl.multiple_of((i - 1 - jj) * tq, tq)
                return step(carry, kn_ref[0, pl.ds(k0, tq), :], vn_ref[0, pl.ds(k0, tq), :], tri_q)

            if nq > 1:
                carry = lax.fori_loop(0, i, new_body, carry)
            if tail:
                k0 = n_full * ATTN_BLOCK
                carry = step(carry, kp_ref[0, k0:k0 + tail, :], vp_ref[0, k0:k0 + tail, :], tri_tail)

            def past_body(cc, carry):
                k0 = pl.multiple_of((n_full - 1 - cc) * ATTN_BLOCK, ATTN_BLOCK)
                return step(carry, kp_ref[0, pl.ds(k0, ATTN_BLOCK), :],
                            vp_ref[0, pl.ds(k0, ATTN_BLOCK), :], tri_full)

            if n_full:
                carry = lax.fori_loop(0, n_full, past_body, carry)
            outs.append(carry[1])
        o_ref[0, pl.ds(q0, tq), :] = jnp.where(masks[0], outs[0], outs[1])

    if nq == 1:
        q_block(0)
    else:
        lax.fori_loop(0, nq, lambda i, _: (q_block(i), 0)[1], 0)


def _attn_specs(bsz, lq, p_len, past_shared):
    pair_new = pl.BlockSpec((1, lq, LANES), lambda b, p: (b, 0, p))
    if past_shared:
        pair_past = pl.BlockSpec((1, p_len, LANES), lambda b, p: (0, 0, p))
    else:
        pair_past = pl.BlockSpec((1, p_len, LANES), lambda b, p: (b, 0, p))
    return pair_new, pair_past


def _fox_attn(q, k_new, v_new, k_past, v_past, cq, cn, cp):
    bsz, lq, _ = q.shape
    p_len = k_past.shape[1]
    tq = min(ATTN_BLOCK, lq)
    pair_new, pair_past = _attn_specs(bsz, lq, p_len, k_past.shape[0] == 1)
    c_spec = lambda a: pl.BlockSpec((1, 1) + a.shape[2:], lambda b, p: (b, p) + (0,) * (a.ndim - 2))
    return pl.pallas_call(
        functools.partial(_fox_kernel, tq=tq, p_len=p_len),
        out_shape=jax.ShapeDtypeStruct(q.shape, F32),
        grid=(bsz, N_PAIRS),
        in_specs=[pair_new, pair_new, pair_new, pair_past, pair_past, c_spec(cq), c_spec(cn), c_spec(cp)],
        out_specs=pair_new,
        compiler_params=_cparams(2, VMEM_LIMIT),
        name="fox_attn",
    )(q, k_new, v_new, k_past, v_past, cq, cn, cp)


def _sb_attn(q, k_new, v_new, k_past, v_past):
    bsz, lq, _ = q.shape
    p_len = k_past.shape[1]
    tq = min(ATTN_BLOCK, lq)
    pair_new, pair_past = _attn_specs(bsz, lq, p_len, k_past.shape[0] == 1)
    return pl.pallas_call(
        functools.partial(_sb_kernel, tq=tq, p_len=p_len),
        out_shape=jax.ShapeDtypeStruct(q.shape, F32),
        grid=(bsz, N_PAIRS),
        in_specs=[pair_new, pair_new, pair_new, pair_past, pair_past],
        out_specs=pair_new,
        compiler_params=_cparams(2, VMEM_LIMIT),
        name="sb_attn",
    )(q, k_new, v_new, k_past, v_past)


def _topk_rows(vals, payload, k):
    n = vals.shape[0]
    rio = lax.broadcasted_iota(jnp.int32, vals.shape, 0)
    out_v, out_p = [], []
    for _ in range(k):
        mx = jnp.max(vals, axis=0, keepdims=True)
        pos = jnp.min(jnp.where(vals == mx, rio, n), axis=0, keepdims=True)
        sel = rio == pos
        out_v.append(mx)
        if payload is None:
            out_p.append(pos)
        else:
            out_p.append(jnp.sum(jnp.where(sel, payload, 0), axis=0, keepdims=True))
        vals = jnp.where(sel, -jnp.inf, vals)
    return jnp.concatenate(out_v, axis=0), jnp.concatenate(out_p, axis=0)


def _mix_route_kernel(x_ref, of_ref, os_ref, ga_ref, gb_ref, wof_ref, wos_ref, wout_ref, gf_ref,
                      wq_ref, k1_ref, k2_ref, h_ref, n2_ref, idx_ref, gate_ref):
    yf = _dot(of_ref[...].astype(BF16), wof_ref[...])
    ys = _dot(os_ref[...].astype(BF16), wos_ref[...])
    merged = ga_ref[...] * yf + gb_ref[...] * ys
    h = x_ref[...] + _dot(merged.astype(BF16), wout_ref[...])
    h_ref[...] = h
    ms = jnp.mean(h * h, axis=-1, keepdims=True)
    n2 = h * lax.rsqrt(ms + NORM_EPS) * gf_ref[...]
    n2_ref[...] = n2
    qp = _dot(n2.astype(BF16), wq_ref[...]).astype(BF16)
    k1 = k1_ref[...]
    k2 = k2_ref[...]
    tl = x_ref.shape[0]
    for c0 in range(0, tl, LANES):
        for hd in range(PEER_HEADS):
            base = hd * 2 * PEER_HALF
            q1 = qp[c0:c0 + LANES, base:base + PEER_HALF]
            q2 = qp[c0:c0 + LANES, base + PEER_HALF:base + 2 * PEER_HALF]
            s1 = _dot_nt(k1, q1)
            s2 = _dot_nt(k2, q2)
            v1, i1 = _topk_rows(s1, None, PEER_TOPK)
            v2, i2 = _topk_rows(s2, None, PEER_TOPK)
            cand = jnp.concatenate([v1[a:a + 1, :] + v2 for a in range(PEER_TOPK)], axis=0)
            eid = jnp.concatenate([i1[a:a + 1, :] * PEER_N_KEYS + i2 for a in range(PEER_TOPK)], axis=0)
            sc, e = _topk_rows(cand, eid, PEER_TOPK)
            ex = jnp.exp(sc - jnp.max(sc, axis=0, keepdims=True))
            gate = ex / jnp.sum(ex, axis=0, keepdims=True)
            r0 = hd * PEER_TOPK
            idx_ref[r0:r0 + PEER_TOPK, c0:c0 + LANES] = e * ROWS_PER_EXPERT
            gate_ref[r0:r0 + PEER_TOPK, c0:c0 + LANES] = gate


def _mix_route(x, o_fox, o_sb, ga, gb, p):
    t = x.shape[0]
    tl = min(TOKEN_BLOCK, t)
    assert t % tl == 0 and tl % LANES == 0
    row = lambda w: pl.BlockSpec((tl, w), lambda i: (i, 0))
    full = lambda a: pl.BlockSpec(a.shape, lambda i: (0,) * a.ndim)
    colblk = pl.BlockSpec((PEER_PICKS, tl), lambda i: (0, i))
    consts_a = (p["w_o_fox"], p["w_o_sb"], p["w_out"], p["g_ffn"], p["w_peer_q"], p["keys1"], p["keys2"])
    return pl.pallas_call(
        _mix_route_kernel,
        out_shape=(jax.ShapeDtypeStruct((t, D_MODEL), F32), jax.ShapeDtypeStruct((t, D_MODEL), F32),
                   jax.ShapeDtypeStruct((PEER_PICKS, t), jnp.int32), jax.ShapeDtypeStruct((PEER_PICKS, t), F32)),
        grid=(t // tl,),
        in_specs=[row(D_MODEL), row(WIDTH), row(WIDTH), row(D_MODEL), row(D_MODEL)] + [full(c) for c in consts_a],
        out_specs=(row(D_MODEL), row(D_MODEL), colblk, colblk),
        compiler_params=_cparams(1, VMEM_LIMIT),
        name="mix_route",
    )(x, o_fox, o_sb, ga, gb, *consts_a)


def _unpack_pair(words):
    lo = pltpu.bitcast(words << 16, F32)
    hi = pltpu.bitcast(words & jnp.int32(-65536), F32)
    return lo, hi


def _peer_act_kernel(idx_ref, x_ref, gate_ref, tbl_ref, w_ref, prod_ref):
    tb = x_ref.shape[0]
    lane = lax.broadcasted_iota(jnp.int32, (PEER_PICKS, tb), 1)

    def token(t, dots):
        xt = x_ref[t]
        x_lo = xt[:ROWS_PER_EXPERT]
        x_hi = xt[ROWS_PER_EXPERT:]
        for j in range(PEER_PICKS):
            r0 = pl.multiple_of(idx_ref[j, t], ROWS_PER_EXPERT)
            lo, hi = _unpack_pair(tbl_ref[pl.ds(r0, ROWS_PER_EXPERT), :])
            prod_ref[j * ROWS_PER_EXPERT:(j + 1) * ROWS_PER_EXPERT, :] = lo * x_lo + hi * x_hi
        part = prod_ref[pl.ds(0, PEER_PICKS, stride=ROWS_PER_EXPERT), :]
        for r in range(1, ROWS_PER_EXPERT):
            part = part + prod_ref[pl.ds(r, PEER_PICKS, stride=ROWS_PER_EXPERT), :]
        d = jnp.sum(part, axis=-1, keepdims=True)
        return jnp.where(lane == t, d, dots)

    dots = lax.fori_loop(0, tb, token, jnp.zeros((PEER_PICKS, tb), F32))
    act = 0.5 * dots * (1.0 + lax.erf(dots * (2.0 ** -0.5)))
    w_ref[...] = gate_ref[...] * act


def _peer_out_kernel(idx_ref, w_ref, h_ref, tbl_ref, o_ref):
    tb = h_ref.shape[0]
    n_acc = 4

    def token(t, _):
        acc_lo = [jnp.zeros((ROWS_PER_EXPERT, LANES), F32) for _ in range(n_acc)]
        acc_hi = [jnp.zeros((ROWS_PER_EXPERT, LANES), F32) for _ in range(n_acc)]
        for j in range(PEER_PICKS):
            r0 = pl.multiple_of(idx_ref[j, t], ROWS_PER_EXPERT)
            lo, hi = _unpack_pair(tbl_ref[pl.ds(r0, ROWS_PER_EXPERT), :])
            w = w_ref[j, t]
            acc_lo[j % n_acc] = acc_lo[j % n_acc] + w * lo
            acc_hi[j % n_acc] = acc_hi[j % n_acc] + w * hi
        lo = (acc_lo[0] + acc_lo[1]) + (acc_lo[2] + acc_lo[3])
        hi = (acc_hi[0] + acc_hi[1]) + (acc_hi[2] + acc_hi[3])
        o_ref[t] = h_ref[t] + jnp.concatenate([lo, hi], axis=0)
        return 0

    lax.fori_loop(0, tb, token, 0)


def _peer_specs(t):
    tb = PEER_TOKEN_BLOCK
    assert t % tb == 0
    smem = pl.BlockSpec((PEER_PICKS, tb), lambda i: (0, i), memory_space=pltpu.SMEM)
    vcol = pl.BlockSpec((PEER_PICKS, tb), lambda i: (0, i))
    tok = pl.BlockSpec((tb, SUBLANES, LANES), lambda i: (i, 0, 0))
    return tb, smem, vcol, tok


def _table_spec(tbl):
    return pl.BlockSpec(tbl.shape, lambda i: (0, 0), pipeline_mode=pl.Buffered(1))


def _peer_act(idx_t, n2_r, gate_t, tbl_u):
    t = n2_r.shape[0]
    tb, smem, vcol, tok = _peer_specs(t)
    return pl.pallas_call(
        _peer_act_kernel,
        out_shape=jax.ShapeDtypeStruct((PEER_PICKS, t), F32),
        grid=(t // tb,),
        in_specs=[smem, tok, vcol, _table_spec(tbl_u)],
        out_specs=vcol,
        scratch_shapes=[pltpu.VMEM((PEER_PICKS * ROWS_PER_EXPERT, LANES), F32)],
        compiler_params=_cparams(1, VMEM_LIMIT),
        name="peer_act",
    )(idx_t, n2_r, gate_t, tbl_u)


def _peer_out(idx_t, w_t, h_r, tbl_v):
    t = h_r.shape[0]
    tb, smem, vcol, tok = _peer_specs(t)
    return pl.pallas_call(
        _peer_out_kernel,
        out_shape=jax.ShapeDtypeStruct(h_r.shape, F32),
        grid=(t // tb,),
        in_specs=[smem, smem, tok, _table_spec(tbl_v)],
        out_specs=tok,
        compiler_params=_cparams(1, VMEM_LIMIT),
        name="peer_out",
    )(idx_t, w_t, h_r, tbl_v)


def _pack_table(tbl):
    bits = lax.bitcast_convert_type(tbl.astype(BF16), jnp.uint16).astype(jnp.uint32)
    half = D_MODEL // 2
    packed = bits[:, :half] | (bits[:, half:] << 16)
    return lax.bitcast_convert_type(packed, jnp.int32).reshape(-1, LANES)


def _prep_params(norm_mix, w_in, b_forget, fox_q_norm, fox_k_norm, w_o_fox, w_o_sb, w_out,
                 norm_ffn, w_peer_q, keys1, keys2, expert_u, expert_v):
    w = w_in.astype(BF16)
    o_f = 3 * WIDTH
    o_b = o_f + N_HEADS
    o_g = o_b + 3 * WIDTH
    pad = LANES - N_HEADS
    head = lax.broadcasted_iota(jnp.int32, (WIDTH, WIDTH), 0) // HEAD_DIM
    head_t = lax.broadcasted_iota(jnp.int32, (WIDTH, WIDTH), 1) // HEAD_DIM
    return {
        "g_mix": norm_mix.reshape(1, D_MODEL),
        "w_a": w[:, :o_f],
        "w_f": jnp.pad(w[:, o_f:o_b], ((0, 0), (0, pad))),
        "w_b": w[:, o_b:o_g],
        "w_g": w[:, o_g:],
        "b_f": jnp.pad(b_forget.reshape(1, N_HEADS), ((0, 0), (0, pad))),
        "qn": jnp.tile(fox_q_norm.reshape(1, HEAD_DIM), (1, N_HEADS)),
        "kn": jnp.tile(fox_k_norm.reshape(1, HEAD_DIM), (1, N_HEADS)),
        "bd": (head == head_t).astype(BF16),
        "w_o_fox": w_o_fox.astype(BF16),
        "w_o_sb": w_o_sb.astype(BF16),
        "w_out": w_out.astype(BF16),
        "g_ffn": norm_ffn.reshape(1, D_MODEL),
        "w_peer_q": w_peer_q.astype(BF16),
        "keys1": keys1.astype(BF16),
        "keys2": keys2.astype(BF16),
        "tbl_u": _pack_table(expert_u),
        "tbl_v": _pack_table(expert_v),
    }


def _forget_bias_layouts(lf_past, lf_new, tq):
    bsz, lq, _ = lf_new.shape
    p_len = lf_past.shape[1]
    total = p_len + lq
    lp = -(-total // LANES) * LANES
    lf_all = jnp.concatenate([jnp.broadcast_to(lf_past, (bsz, p_len, N_HEADS)), lf_new], axis=1)
    rows = jnp.pad(jnp.transpose(lf_all, (0, 2, 1)), ((0, 0), (0, 0), (0, lp - total)))
    c = _logf_cumsum(rows.reshape(bsz * N_HEADS, lp)).reshape(bsz, N_PAIRS, 2, lp)
    c_new = c[..., p_len:total]
    cq = jnp.transpose(c_new, (0, 1, 3, 2))
    cn = jnp.transpose(c_new.reshape(bsz, N_PAIRS, 2, lq // tq, tq), (0, 1, 3, 2, 4))
    pp = -(-p_len // LANES) * LANES
    c_past = jnp.pad(c[..., :p_len], ((0, 0), (0, 0), (0, 0), (0, pp - p_len)))
    cp = jnp.transpose(c_past.reshape(bsz, N_PAIRS, 2, pp // LANES, LANES), (0, 1, 3, 2, 4))
    return cq, cn, cp


def _group_forward(x, past, p):
    bsz, lq, _ = x.shape
    t = bsz * lq
    x2 = x.reshape(t, D_MODEL)
    fq, fk, fv, lf, sq, sk, sv, ga, gb = _in_proj(x2, p)
    b3 = lambda a: a.reshape(bsz, lq, a.shape[-1])
    pk, pv, plf, psk, psv = past
    tq = min(ATTN_BLOCK, lq)
    cq, cn, cp = _forget_bias_layouts(plf, b3(lf), tq)
    o_fox = _fox_attn(b3(fq), b3(fk), b3(fv), pk, pv, cq, cn, cp)
    o_sb = _sb_attn(b3(sq), b3(sk), b3(sv), psk, psv)
    h, n2, idx_t, gate_t = _mix_route(x2, o_fox.reshape(t, WIDTH), o_sb.reshape(t, WIDTH), ga, gb, p)
    tile = lambda a: a.reshape(t, SUBLANES, LANES)
    w_t = _peer_act(idx_t, tile(n2), gate_t, p["tbl_u"])
    out = _peer_out(idx_t, w_t, tile(h), p["tbl_v"])
    return out.reshape(bsz, lq, D_MODEL), (b3(fk), b3(fv), b3(lf), b3(sk), b3(sv))


def kernel(x_prompt, x_sample, cache_fox_k, cache_fox_v, cache_fox_logf, cache_sb_k, cache_sb_v,
           meta_tokens, norm_mix, w_in, b_forget, fox_q_norm, fox_k_norm, w_o_fox, w_o_sb, w_out,
           norm_ffn, w_peer_q, peer_sub_keys_1, peer_sub_keys_2, expert_u, expert_v):
    assert norm_mix.shape[0] == 1, "single-layer stack"
    p = _prep_params(norm_mix[0], w_in[0], b_forget[0], fox_q_norm[0], fox_k_norm[0], w_o_fox[0],
                     w_o_sb[0], w_out[0], norm_ffn[0], w_peer_q[0], peer_sub_keys_1[0],
                     peer_sub_keys_2[0], expert_u[0], expert_v[0])
    bsz = x_prompt.shape[0]
    n_meta = meta_tokens.shape[0]

    _, mk, mv, mlf, _, msk, msv, _, _ = _in_proj(meta_tokens.astype(x_prompt.dtype), p)
    meta_past = (mk[None], mv[None], mlf[None], msk[None], msv[None])
    y_prompt, rows_p = _group_forward(x_prompt, meta_past, p)

    def with_meta(meta_rows, new_rows, tail_shape):
        full = jnp.concatenate([jnp.broadcast_to(meta_rows[None], (bsz,) + meta_rows.shape), new_rows], axis=1)
        return full.reshape((1, bsz, n_meta + new_rows.shape[1]) + tail_shape)

    hd = (N_HEADS, HEAD_DIM)
    out_p = (with_meta(mk, rows_p[0], hd), with_meta(mv, rows_p[1], hd), with_meta(mlf, rows_p[2], (N_HEADS,)),
             with_meta(msk, rows_p[3], hd), with_meta(msv, rows_p[4], hd))

    dbsz, plen = cache_fox_k.shape[1], cache_fox_k.shape[2]
    flat = lambda c: c[0].reshape(dbsz, plen, -1)
    sample_past = (flat(cache_fox_k), flat(cache_fox_v), flat(cache_fox_logf), flat(cache_sb_k), flat(cache_sb_v))
    y_sample, rows_s = _group_forward(x_sample, sample_past, p)
    lq = x_sample.shape[1]
    out_s = (rows_s[0].reshape(1, dbsz, lq, *hd), rows_s[1].reshape(1, dbsz, lq, *hd),
             rows_s[2].reshape(1, dbsz, lq, N_HEADS), rows_s[3].reshape(1, dbsz, lq, *hd),
             rows_s[4].reshape(1, dbsz, lq, *hd))
    return (y_prompt, y_sample) + out_p + out_s
```

```python
import functools

import jax
import jax.numpy as jnp
from jax import lax
from jax.experimental import pallas as pl
from jax.experimental.pallas import tpu as pltpu

F32 = jnp.float32
BF16 = jnp.bfloat16

D_MODEL = 1024
HEAD_DIM = 64
N_HEADS = 8
WIDTH = N_HEADS * HEAD_DIM
N_PAIRS = N_HEADS // 2
NORM_EPS = 1e-6
MASK_VALUE = -1e30
ATTN_SCALE = HEAD_DIM ** -0.5

PEER_HEADS = 8
PEER_N_KEYS = 128
PEER_HALF = 128
PEER_TOPK = 16
PEER_PICKS = PEER_HEADS * PEER_TOPK

LANES = 128
SUBLANES = 8
ROWS_PER_EXPERT = D_MODEL // 2 // LANES

TOKEN_BLOCK = 256
ATTN_BLOCK = 256
PAST_CHUNK_FOX = 512
PAST_CHUNK_SB = 128
PEER_TOKEN_BLOCK = 128
VMEM_LIMIT = 56 * 1024 * 1024


def _cparams(n_grid, vmem=None):
    return pltpu.CompilerParams(
        dimension_semantics=("arbitrary",) * n_grid,
        vmem_limit_bytes=vmem,
    )


def _dot(a, b):
    return jnp.dot(a, b, preferred_element_type=F32)


def _dot_nt(a, b):
    return lax.dot_general(a, b, (((1,), (1,)), ((), ())), preferred_element_type=F32)


def _split_dot(a, b_bf16, terms):
    out = None
    rem = a
    for i in range(terms):
        part = rem.astype(BF16)
        d = _dot(part, b_bf16)
        out = d if out is None else out + d
        if i + 1 < terms:
            rem = rem - part.astype(F32)
    return out


def _log_sigmoid(x):
    return jnp.minimum(x, 0.0) - jnp.log1p(jnp.exp(-jnp.abs(x)))


def _sigmoid(x):
    return 1.0 / (1.0 + jnp.exp(-x))


def _in_proj_kernel(x_ref, g_ref, wa_ref, wf_ref, wb_ref, wg_ref, bf_ref, qn_ref, kn_ref, bd_ref,
                    fq_ref, fk_ref, fv_ref, lf_ref, sq_ref, sk_ref, sv_ref, ga_ref, gb_ref):
    x = x_ref[...]
    ms = jnp.mean(x * x, axis=-1, keepdims=True)
    n = (x * lax.rsqrt(ms + NORM_EPS) * g_ref[...]).astype(BF16)

    a = _dot(n, wa_ref[...])
    bd = bd_ref[...]

    def head_norm(t, gain):
        msq = _split_dot(t * t, bd, 2) * (1.0 / HEAD_DIM)
        return t * lax.rsqrt(msq + NORM_EPS) * gain

    fq_ref[...] = head_norm(a[:, :WIDTH], qn_ref[...])
    fk_ref[...] = head_norm(a[:, WIDTH:2 * WIDTH], kn_ref[...])
    fv_ref[...] = a[:, 2 * WIDTH:]

    f = _dot(n, wf_ref[...]) + bf_ref[...]
    lf_ref[...] = _log_sigmoid(f)[:, :N_HEADS]

    b = _dot(n, wb_ref[...])
    sq_ref[...] = b[:, :WIDTH]
    sk_ref[...] = b[:, WIDTH:2 * WIDTH]
    sv_ref[...] = b[:, 2 * WIDTH:]

    g = _dot(n, wg_ref[...])
    ga_ref[...] = _sigmoid(g[:, :D_MODEL])
    gb_ref[...] = _sigmoid(g[:, D_MODEL:])


def _in_proj(x, p):
    t = x.shape[0]
    tl = min(TOKEN_BLOCK, t)
    assert t % tl == 0
    row = lambda w: pl.BlockSpec((tl, w), lambda i: (i, 0))
    full = lambda a: pl.BlockSpec(a.shape, lambda i: (0,) * a.ndim)
    consts = (p["g_mix"], p["w_a"], p["w_f"], p["w_b"], p["w_g"], p["b_f"], p["qn"], p["kn"], p["bd"])
    widths = (WIDTH, WIDTH, WIDTH, N_HEADS, WIDTH, WIDTH, WIDTH, D_MODEL, D_MODEL)
    return pl.pallas_call(
        _in_proj_kernel,
        out_shape=tuple(jax.ShapeDtypeStruct((t, w), F32) for w in widths),
        grid=(t // tl,),
        in_specs=[row(D_MODEL)] + [full(c) for c in consts],
        out_specs=tuple(row(w) for w in widths),
        compiler_params=_cparams(1, VMEM_LIMIT),
        name="in_proj",
    )(x, *consts)


def _cumsum_kernel(x_ref, o_ref):
    n_chunks = x_ref.shape[1] // LANES
    r = lax.broadcasted_iota(jnp.int32, (LANES, LANES), 0)
    c = lax.broadcasted_iota(jnp.int32, (LANES, LANES), 1)
    upper = jnp.where(r <= c, 1.0, 0.0).astype(BF16)
    carry = jnp.zeros((x_ref.shape[0], 1), F32)
    for k in range(n_chunks):
        y = _split_dot(x_ref[:, k * LANES:(k + 1) * LANES], upper, 3) + carry
        o_ref[:, k * LANES:(k + 1) * LANES] = y
        carry = y[:, LANES - 1:LANES]


def _logf_cumsum(lf_rows):
    return pl.pallas_call(
        _cumsum_kernel,
        out_shape=jax.ShapeDtypeStruct(lf_rows.shape, F32),
        name="logf_cumsum",
    )(lf_rows)


def _head_masks():
    lane = lax.broadcasted_iota(jnp.int32, (1, LANES), 1)
    return lane < HEAD_DIM, lane >= HEAD_DIM


def _stack_heads(x):
    m0, m1 = _head_masks()
    return jnp.concatenate([jnp.where(m0, x, 0.0), jnp.where(m1, x, 0.0)], axis=0)


def _unstack_heads(y, tq):
    m0, _ = _head_masks()
    return jnp.where(m0, y[:tq], y[tq:])


def _stacked_pos(tq, tk):
    r = lax.broadcasted_iota(jnp.int32, (2 * tq, tk), 0)
    c = lax.broadcasted_iota(jnp.int32, (2 * tq, tk), 1)
    return jnp.where(r >= tq, r - tq, r), c


def _static_chunks(n, size):
    out = [(s, size) for s in range(0, n - size + 1, size)]
    done = len(out) * size
    if done < n:
        out.append((done, n - done))
    return out


def _tri_strict(n):
    r = lax.broadcasted_iota(jnp.int32, (n, n), 0)
    c = lax.broadcasted_iota(jnp.int32, (n, n), 1)
    return jnp.where(r > c, 1.0, 0.0).astype(BF16)


def _softmax_step(carry, qs, cq, is_head1, segments):
    scores = []
    for kb, _, ck, valid in segments:
        bias = jnp.where(is_head1, ck[1:2, :], ck[0:1, :])
        s = _dot_nt(qs, kb.astype(BF16)) + (cq - bias)
        if valid is not None:
            s = jnp.where(valid, s, MASK_VALUE)
        scores.append(s)
    m_new = functools.reduce(jnp.maximum, [jnp.max(s, axis=-1, keepdims=True) for s in scores])
    if carry is not None:
        m_old, l_old, acc_old = carry
        m_new = jnp.maximum(m_old, m_new)
    l = None
    acc = None
    for s, (_, vb, _, _) in zip(scores, segments):
        pr = jnp.exp(s - m_new)
        ls = jnp.sum(pr, axis=-1, keepdims=True)
        pv = _dot(pr.astype(BF16), vb.astype(BF16))
        l = ls if l is None else l + ls
        acc = pv if acc is None else acc + pv
    if carry is not None:
        alpha = jnp.exp(m_old - m_new)
        l = alpha * l_old + l
        acc = alpha * acc_old + acc
    return m_new, l, acc


def _fox_kernel(q_ref, kn_ref, vn_ref, kp_ref, vp_ref, cq_ref, cn_ref, cp_ref, o_ref, *, tq, p_len):
    lq = q_ref.shape[1]
    nq = lq // tq
    row, col = _stacked_pos(tq, tq)
    causal = col <= row
    is_head1 = lax.broadcasted_iota(jnp.int32, (2 * tq, 1), 0) >= tq
    past_size = p_len if nq > 1 else PAST_CHUNK_FOX
    past = [(kp_ref[0, s:s + n, :], vp_ref[0, s:s + n, :], cp_ref[0, 0, :, s:s + n], None)
            for s, n in _static_chunks(p_len, past_size)]

    def q_block(i):
        q0 = i * tq if nq == 1 else pl.multiple_of(i * tq, tq)
        qs = _stack_heads(q_ref[0, pl.ds(q0, tq), :] * ATTN_SCALE).astype(BF16)
        cqb = cq_ref[0, 0, pl.ds(q0, tq), :]
        cq = jnp.concatenate([cqb[:, 0:1], cqb[:, 1:2]], axis=0)
        diag = (kn_ref[0, pl.ds(q0, tq), :], vn_ref[0, pl.ds(q0, tq), :], cn_ref[0, 0, i], causal)
        carry = _softmax_step(None, qs, cq, is_head1, past + [diag])

        def new_body(j, carry):
            k0 = pl.multiple_of(j * tq, tq)
            seg = (kn_ref[0, pl.ds(k0, tq), :], vn_ref[0, pl.ds(k0, tq), :], cn_ref[0, 0, j], None)
            return _softmax_step(carry, qs, cq, is_head1, [seg])

        if nq > 1:
            carry = lax.fori_loop(0, i, new_body, carry)
        _, l, acc = carry
        o_ref[0, pl.ds(q0, tq), :] = _unstack_heads(acc / l, tq)

    if nq == 1:
        q_block(0)
    else:
        lax.fori_loop(0, nq, lambda i, _: (q_block(i), 0)[1], 0)


def _sb_kernel(q_ref, kn_ref, vn_ref, kp_ref, vp_ref, o_ref, *, tq, p_len):
    lq = q_ref.shape[1]
    nq = lq // tq
    row, col = _stacked_pos(tq, tq)
    strictly_before = col < row
    tri_q = _tri_strict(tq)

    def local(qs, kb, tri_m, valid=None):
        z = _dot_nt(qs, kb.astype(BF16))
        m = _log_sigmoid(-z)
        if valid is not None:
            m = jnp.where(valid, m, 0.0)
        tail_in = _split_dot(m, tri_m, 2)
        return z, m, tail_in, tail_in[:, 0:1] + m[:, 0:1]

    def weights(z, m, tail_in, run, valid=None):
        a = jnp.exp(z + m + tail_in) if run is None else jnp.exp(z + m + tail_in + run)
        if valid is not None:
            a = jnp.where(valid, a, 0.0)
        return a.astype(BF16)

    def q_block(i):
        q0 = i * tq if nq == 1 else pl.multiple_of(i * tq, tq)
        qs = _stack_heads(q_ref[0, pl.ds(q0, tq), :] * ATTN_SCALE).astype(BF16)
        z, m, tail_in, tot = local(qs, kn_ref[0, pl.ds(q0, tq), :], tri_q, strictly_before)
        acc = _dot(weights(z, m, tail_in, None, strictly_before), vn_ref[0, pl.ds(q0, tq), :].astype(BF16))
        run = tot

        if nq > 1:
            assert p_len <= ATTN_BLOCK
            zp, mp, tp, _ = local(qs, kp_ref[0], _tri_strict(p_len))
            past_acc = _dot(weights(zp, mp, tp, None), vp_ref[0].astype(BF16))

            def new_body(jj, carry):
                run, acc = carry
                k0 = pl.multiple_of((i - 1 - jj) * tq, tq)
                z, m, tail_in, tot = local(qs, kn_ref[0, pl.ds(k0, tq), :], tri_q)
                acc = acc + _dot(weights(z, m, tail_in, run), vn_ref[0, pl.ds(k0, tq), :].astype(BF16))
                return run + tot, acc

            run, acc = lax.fori_loop(0, i, new_body, (run, acc))
            acc = acc + jnp.exp(run) * past_acc
        else:
            chunks = _static_chunks(p_len, PAST_CHUNK_SB)
            tris = {n: _tri_strict(n) for n in {n for _, n in chunks}}
            for s, n in reversed(chunks):
                z, m, tail_in, tot = local(qs, kp_ref[0, s:s + n, :], tris[n])
                acc = acc + _dot(weights(z, m, tail_in, run), vp_ref[0, s:s + n, :].astype(BF16))
                run = run + tot
        o_ref[0, pl.ds(q0, tq), :] = _unstack_heads(acc, tq)

    if nq == 1:
        q_block(0)
    else:
        lax.fori_loop(0, nq, lambda i, _: (q_block(i), 0)[1], 0)


def _attn_specs(bsz, lq, p_len, past_shared):
    pair_new = pl.BlockSpec((1, lq, LANES), lambda b, p: (b, 0, p))
    if past_shared:
        pair_past = pl.BlockSpec((1, p_len, LANES), lambda b, p: (0, 0, p))
    else:
        pair_past = pl.BlockSpec((1, p_len, LANES), lambda b, p: (b, 0, p))
    return pair_new, pair_past


def _fox_attn(q, k_new, v_new, k_past, v_past, cq, cn, cp):
    bsz, lq, _ = q.shape
    p_len = k_past.shape[1]
    tq = min(ATTN_BLOCK, lq)
    pair_new, pair_past = _attn_specs(bsz, lq, p_len, k_past.shape[0] == 1)
    c_spec = lambda a: pl.BlockSpec((1, 1) + a.shape[2:], lambda b, p: (b, p) + (0,) * (a.ndim - 2))
    return pl.pallas_call(
        functools.partial(_fox_kernel, tq=tq, p_len=p_len),
        out_shape=jax.ShapeDtypeStruct(q.shape, F32),
        grid=(bsz, N_PAIRS),
        in_specs=[pair_new, pair_new, pair_new, pair_past, pair_past, c_spec(cq), c_spec(cn), c_spec(cp)],
        out_specs=pair_new,
        compiler_params=_cparams(2, VMEM_LIMIT),
        name="fox_attn",
    )(q, k_new, v_new, k_past, v_past, cq, cn, cp)


def _sb_attn(q, k_new, v_new, k_past, v_past):
    bsz, lq, _ = q.shape
    p_len = k_past.shape[1]
    tq = min(ATTN_BLOCK, lq)
    pair_new, pair_past = _attn_specs(bsz, lq, p_len, k_past.shape[0] == 1)
    return pl.pallas_call(
        functools.partial(_sb_kernel, tq=tq, p_len=p_len),
        out_shape=jax.ShapeDtypeStruct(q.shape, F32),
        grid=(bsz, N_PAIRS),
        in_specs=[pair_new, pair_new, pair_new, pair_past, pair_past],
        out_specs=pair_new,
        compiler_params=_cparams(2, VMEM_LIMIT),
        name="sb_attn",
    )(q, k_new, v_new, k_past, v_past)


def _topk_rows(vals, payload, k):
    n = vals.shape[0]
    rio = lax.broadcasted_iota(jnp.int32, vals.shape, 0)
    out_v, out_p = [], []
    for _ in range(k):
        mx = jnp.max(vals, axis=0, keepdims=True)
        pos = jnp.min(jnp.where(vals == mx, rio, n), axis=0, keepdims=True)
        sel = rio == pos
        out_v.append(mx)
        if payload is None:
            out_p.append(pos)
        else:
            out_p.append(jnp.sum(jnp.where(sel, payload, 0), axis=0, keepdims=True))
        vals = jnp.where(sel, -jnp.inf, vals)
    return jnp.concatenate(out_v, axis=0), jnp.concatenate(out_p, axis=0)


def _pair_rows(first, second):
    half = PEER_TOPK // 2
    rows = [first[0:1, :] + second]
    rows += [first[a:a + 1, :] + second[:half] for a in range(1, half)]
    rows.append(first[half:, :] + second[0:1, :])
    return jnp.concatenate(rows, axis=0)


def _mix_route_kernel(x_ref, of_ref, os_ref, ga_ref, gb_ref, wof_ref, wos_ref, wout_ref, gf_ref,
                      wq_ref, k1_ref, k2_ref, h_ref, n2_ref, idx_ref, gate_ref):
    yf = _dot(of_ref[...].astype(BF16), wof_ref[...])
    ys = _dot(os_ref[...].astype(BF16), wos_ref[...])
    merged = ga_ref[...] * yf + gb_ref[...] * ys
    h = x_ref[...] + _dot(merged.astype(BF16), wout_ref[...])
    h_ref[...] = h
    ms = jnp.mean(h * h, axis=-1, keepdims=True)
    n2 = h * lax.rsqrt(ms + NORM_EPS) * gf_ref[...]
    n2_ref[...] = n2
    qp = _dot(n2.astype(BF16), wq_ref[...]).astype(BF16)
    k1 = k1_ref[...]
    k2 = k2_ref[...]
    tl = x_ref.shape[0]
    for c0 in range(0, tl, LANES):
        rows = []
        for hd in range(PEER_HEADS):
            base = hd * 2 * PEER_HALF
            q1 = qp[c0:c0 + LANES, base:base + PEER_HALF]
            q2 = qp[c0:c0 + LANES, base + PEER_HALF:base + 2 * PEER_HALF]
            s1 = _dot_nt(k1, q1)
            s2 = _dot_nt(k2, q2)
            v1, i1 = _topk_rows(s1, None, PEER_TOPK)
            v2, i2 = _topk_rows(s2, None, PEER_TOPK)
            sc, e = _topk_rows(_pair_rows(v1, v2), _pair_rows(i1 * PEER_N_KEYS, i2), PEER_TOPK)
            ex = jnp.exp(sc - jnp.max(sc, axis=0, keepdims=True))
            gate = ex / jnp.sum(ex, axis=0, keepdims=True)
            r0 = hd * PEER_TOPK
            rows.append(e * ROWS_PER_EXPERT)
            gate_ref[r0:r0 + PEER_TOPK, c0:c0 + LANES] = gate
        idx_ref[c0:c0 + LANES, :] = jnp.concatenate(rows, axis=0).T


def _mix_route(x, o_fox, o_sb, ga, gb, p):
    t = x.shape[0]
    tl = min(TOKEN_BLOCK, t)
    assert t % tl == 0 and tl % LANES == 0
    row = lambda w: pl.BlockSpec((tl, w), lambda i: (i, 0))
    full = lambda a: pl.BlockSpec(a.shape, lambda i: (0,) * a.ndim)
    colblk = pl.BlockSpec((PEER_PICKS, tl), lambda i: (0, i))
    consts_a = (p["w_o_fox"], p["w_o_sb"], p["w_out"], p["g_ffn"], p["w_peer_q"], p["keys1"], p["keys2"])
    return pl.pallas_call(
        _mix_route_kernel,
        out_shape=(jax.ShapeDtypeStruct((t, D_MODEL), F32), jax.ShapeDtypeStruct((t, D_MODEL), F32),
                   jax.ShapeDtypeStruct((t, PEER_PICKS), jnp.int32), jax.ShapeDtypeStruct((PEER_PICKS, t), F32)),
        grid=(t // tl,),
        in_specs=[row(D_MODEL), row(WIDTH), row(WIDTH), row(D_MODEL), row(D_MODEL)] + [full(c) for c in consts_a],
        out_specs=(row(D_MODEL), row(D_MODEL), row(PEER_PICKS), colblk),
        compiler_params=_cparams(1, VMEM_LIMIT),
        name="mix_route",
    )(x, o_fox, o_sb, ga, gb, *consts_a)


def _unpack_pair(words):
    lo = pltpu.bitcast(words << 16, F32)
    hi = pltpu.bitcast(words & jnp.int32(-65536), F32)
    return lo, hi


def _peer_act_kernel(idx_ref, x_ref, gate_ref, tbl_ref, w_ref, prod_ref):
    tb = x_ref.shape[0]
    lane = lax.broadcasted_iota(jnp.int32, (PEER_PICKS, tb), 1)

    def token(t, dots):
        xt = x_ref[t]
        x_lo = xt[:ROWS_PER_EXPERT]
        x_hi = xt[ROWS_PER_EXPERT:]
        for j in range(PEER_PICKS):
            r0 = pl.multiple_of(idx_ref[t, j], ROWS_PER_EXPERT)
            lo, hi = _unpack_pair(tbl_ref[pl.ds(r0, ROWS_PER_EXPERT), :])
            prod_ref[j * ROWS_PER_EXPERT:(j + 1) * ROWS_PER_EXPERT, :] = lo * x_lo + hi * x_hi
        part = prod_ref[pl.ds(0, PEER_PICKS, stride=ROWS_PER_EXPERT), :]
        for r in range(1, ROWS_PER_EXPERT):
            part = part + prod_ref[pl.ds(r, PEER_PICKS, stride=ROWS_PER_EXPERT), :]
        d = jnp.sum(part, axis=-1, keepdims=True)
        return jnp.where(lane == t, d, dots)

    dots = lax.fori_loop(0, tb, token, jnp.zeros((PEER_PICKS, tb), F32))
    act = 0.5 * dots * (1.0 + lax.erf(dots * (2.0 ** -0.5)))
    w_ref[...] = (gate_ref[...] * act).T


def _peer_out_kernel(idx_ref, w_ref, h_ref, tbl_ref, o_ref):
    tb = h_ref.shape[0]
    n_acc = 4

    def token(t, _):
        acc_lo = [jnp.zeros((ROWS_PER_EXPERT, LANES), F32) for _ in range(n_acc)]
        acc_hi = [jnp.zeros((ROWS_PER_EXPERT, LANES), F32) for _ in range(n_acc)]
        for j in range(PEER_PICKS):
            r0 = pl.multiple_of(idx_ref[t, j], ROWS_PER_EXPERT)
            lo, hi = _unpack_pair(tbl_ref[pl.ds(r0, ROWS_PER_EXPERT), :])
            w = w_ref[t, j]
            acc_lo[j % n_acc] = acc_lo[j % n_acc] + w * lo
            acc_hi[j % n_acc] = acc_hi[j % n_acc] + w * hi
        lo = (acc_lo[0] + acc_lo[1]) + (acc_lo[2] + acc_lo[3])
        hi = (acc_hi[0] + acc_hi[1]) + (acc_hi[2] + acc_hi[3])
        o_ref[t] = h_ref[t] + jnp.concatenate([lo, hi], axis=0)
        return 0

    lax.fori_loop(0, tb, token, 0)


def _peer_specs(t):
    tb = PEER_TOKEN_BLOCK
    assert t % tb == 0
    smem = pl.BlockSpec((tb, PEER_PICKS), lambda i: (i, 0), memory_space=pltpu.SMEM)
    vcol = pl.BlockSpec((PEER_PICKS, tb), lambda i: (0, i))
    vrow = pl.BlockSpec((tb, PEER_PICKS), lambda i: (i, 0))
    tok = pl.BlockSpec((tb, SUBLANES, LANES), lambda i: (i, 0, 0))
    return tb, smem, vcol, vrow, tok


def _table_spec(tbl):
    return pl.BlockSpec(tbl.shape, lambda i: (0, 0), pipeline_mode=pl.Buffered(1))


def _peer_act(idx, n2_r, gate_t, tbl_u):
    t = n2_r.shape[0]
    tb, smem, vcol, vrow, tok = _peer_specs(t)
    return pl.pallas_call(
        _peer_act_kernel,
        out_shape=jax.ShapeDtypeStruct((t, PEER_PICKS), F32),
        grid=(t // tb,),
        in_specs=[smem, tok, vcol, _table_spec(tbl_u)],
        out_specs=vrow,
        scratch_shapes=[pltpu.VMEM((PEER_PICKS * ROWS_PER_EXPERT, LANES), F32)],
        compiler_params=_cparams(1, VMEM_LIMIT),
        name="peer_act",
    )(idx, n2_r, gate_t, tbl_u)


def _peer_out(idx, w, h_r, tbl_v):
    t = h_r.shape[0]
    tb, smem, vcol, vrow, tok = _peer_specs(t)
    return pl.pallas_call(
        _peer_out_kernel,
        out_shape=jax.ShapeDtypeStruct(h_r.shape, F32),
        grid=(t // tb,),
        in_specs=[smem, smem, tok, _table_spec(tbl_v)],
        out_specs=tok,
        compiler_params=_cparams(1, VMEM_LIMIT),
        name="peer_out",
    )(idx, w, h_r, tbl_v)


def _pack_table(tbl):
    bits = lax.bitcast_convert_type(tbl.astype(BF16), jnp.uint16).astype(jnp.uint32)
    half = D_MODEL // 2
    packed = bits[:, :half] | (bits[:, half:] << 16)
    return lax.bitcast_convert_type(packed, jnp.int32).reshape(-1, LANES)


def _prep_params(norm_mix, w_in, b_forget, fox_q_norm, fox_k_norm, w_o_fox, w_o_sb, w_out,
                 norm_ffn, w_peer_q, keys1, keys2, expert_u, expert_v):
    w = w_in.astype(BF16)
    o_f = 3 * WIDTH
    o_b = o_f + N_HEADS
    o_g = o_b + 3 * WIDTH
    pad = LANES - N_HEADS
    head = lax.broadcasted_iota(jnp.int32, (WIDTH, WIDTH), 0) // HEAD_DIM
    head_t = lax.broadcasted_iota(jnp.int32, (WIDTH, WIDTH), 1) // HEAD_DIM
    return {
        "g_mix": norm_mix.reshape(1, D_MODEL),
        "w_a": w[:, :o_f],
        "w_f": jnp.pad(w[:, o_f:o_b], ((0, 0), (0, pad))),
        "w_b": w[:, o_b:o_g],
        "w_g": w[:, o_g:],
        "b_f": jnp.pad(b_forget.reshape(1, N_HEADS), ((0, 0), (0, pad))),
        "qn": jnp.tile(fox_q_norm.reshape(1, HEAD_DIM), (1, N_HEADS)),
        "kn": jnp.tile(fox_k_norm.reshape(1, HEAD_DIM), (1, N_HEADS)),
        "bd": (head == head_t).astype(BF16),
        "w_o_fox": w_o_fox.astype(BF16),
        "w_o_sb": w_o_sb.astype(BF16),
        "w_out": w_out.astype(BF16),
        "g_ffn": norm_ffn.reshape(1, D_MODEL),
        "w_peer_q": w_peer_q.astype(BF16),
        "keys1": keys1.astype(BF16),
        "keys2": keys2.astype(BF16),
        "tbl_u": _pack_table(expert_u),
        "tbl_v": _pack_table(expert_v),
    }


def _forget_bias_layouts(lf_past, lf_new, tq):
    bsz, lq, _ = lf_new.shape
    p_len = lf_past.shape[1]
    total = p_len + lq
    lp = -(-total // LANES) * LANES
    lf_all = jnp.concatenate([jnp.broadcast_to(lf_past, (bsz, p_len, N_HEADS)), lf_new], axis=1)
    rows = jnp.pad(jnp.transpose(lf_all, (0, 2, 1)), ((0, 0), (0, 0), (0, lp - total)))
    c = _logf_cumsum(rows.reshape(bsz * N_HEADS, lp)).reshape(bsz, N_PAIRS, 2, lp)
    c_new = c[..., p_len:total]
    cq = jnp.transpose(c_new, (0, 1, 3, 2))
    cn = jnp.transpose(c_new.reshape(bsz, N_PAIRS, 2, lq // tq, tq), (0, 1, 3, 2, 4))
    cp = c[..., :p_len]
    return cq, cn, cp


def _group_forward(x, past, p):
    bsz, lq, _ = x.shape
    t = bsz * lq
    x2 = x.reshape(t, D_MODEL)
    fq, fk, fv, lf, sq, sk, sv, ga, gb = _in_proj(x2, p)
    b3 = lambda a: a.reshape(bsz, lq, a.shape[-1])
    pk, pv, plf, psk, psv = past
    tq = min(ATTN_BLOCK, lq)
    cq, cn, cp = _forget_bias_layouts(plf, b3(lf), tq)
    o_fox = _fox_attn(b3(fq), b3(fk), b3(fv), pk, pv, cq, cn, cp)
    o_sb = _sb_attn(b3(sq), b3(sk), b3(sv), psk, psv)
    h, n2, idx, gate_t = _mix_route(x2, o_fox.reshape(t, WIDTH), o_sb.reshape(t, WIDTH), ga, gb, p)
    tile = lambda a: a.reshape(t, SUBLANES, LANES)
    w = _peer_act(idx, tile(n2), gate_t, p["tbl_u"])
    out = _peer_out(idx, w, tile(h), p["tbl_v"])
    return out.reshape(bsz, lq, D_MODEL), (b3(fk), b3(fv), b3(lf), b3(sk), b3(sv))


def kernel(x_prompt, x_sample, cache_fox_k, cache_fox_v, cache_fox_logf, cache_sb_k, cache_sb_v,
           meta_tokens, norm_mix, w_in, b_forget, fox_q_norm, fox_k_norm, w_o_fox, w_o_sb, w_out,
           norm_ffn, w_peer_q, peer_sub_keys_1, peer_sub_keys_2, expert_u, expert_v):
    assert norm_mix.shape[0] == 1, "single-layer stack"
    p = _prep_params(norm_mix[0], w_in[0], b_forget[0], fox_q_norm[0], fox_k_norm[0], w_o_fox[0],
                     w_o_sb[0], w_out[0], norm_ffn[0], w_peer_q[0], peer_sub_keys_1[0],
                     peer_sub_keys_2[0], expert_u[0], expert_v[0])
    bsz = x_prompt.shape[0]
    n_meta = meta_tokens.shape[0]

    _, mk, mv, mlf, _, msk, msv, _, _ = _in_proj(meta_tokens.astype(x_prompt.dtype), p)
    meta_past = (mk[None], mv[None], mlf[None], msk[None], msv[None])
    y_prompt, rows_p = _group_forward(x_prompt, meta_past, p)

    def with_meta(meta_rows, new_rows, tail_shape):
        full = jnp.concatenate([jnp.broadcast_to(meta_rows[None], (bsz,) + meta_rows.shape), new_rows], axis=1)
        return full.reshape((1, bsz, n_meta + new_rows.shape[1]) + tail_shape)

    hd = (N_HEADS, HEAD_DIM)
    out_p = (with_meta(mk, rows_p[0], hd), with_meta(mv, rows_p[1], hd), with_meta(mlf, rows_p[2], (N_HEADS,)),
             with_meta(msk, rows_p[3], hd), with_meta(msv, rows_p[4], hd))

    dbsz, plen = cache_fox_k.shape[1], cache_fox_k.shape[2]
    flat = lambda c: c[0].reshape(dbsz, plen, -1)
    sample_past = (flat(cache_fox_k), flat(cache_fox_v), flat(cache_fox_logf), flat(cache_sb_k), flat(cache_sb_v))
    y_sample, rows_s = _group_forward(x_sample, sample_past, p)
    lq = x_sample.shape[1]
    out_s = (rows_s[0].reshape(1, dbsz, lq, *hd), rows_s[1].reshape(1, dbsz, lq, *hd),
             rows_s[2].reshape(1, dbsz, lq, N_HEADS), rows_s[3].reshape(1, dbsz, lq, *hd),
             rows_s[4].reshape(1, dbsz, lq, *hd))
    return (y_prompt, y_sample) + out_p + out_s
```

```python
import functools

import jax
import jax.numpy as jnp
from jax import lax
from jax.experimental import pallas as pl
from jax.experimental.pallas import tpu as pltpu

F32 = jnp.float32
BF16 = jnp.bfloat16

D_MODEL = 1024
HEAD_DIM = 64
N_HEADS = 8
WIDTH = N_HEADS * HEAD_DIM
N_PAIRS = N_HEADS // 2
NORM_EPS = 1e-6
MASK_VALUE = -1e30
ATTN_SCALE = HEAD_DIM ** -0.5

PEER_HEADS = 8
PEER_N_KEYS = 128
PEER_HALF = 128
PEER_TOPK = 16
PEER_PICKS = PEER_HEADS * PEER_TOPK

LANES = 128
SUBLANES = 8
ROWS_PER_EXPERT = D_MODEL // 2 // LANES

TOKEN_BLOCK = 256
ATTN_BLOCK = 256
PAST_CHUNK_FOX = 512
PAST_CHUNK_SB = 128
PEER_TOKEN_BLOCK = 128
ROUTE_CHUNK = 8
VMEM_LIMIT = 56 * 1024 * 1024


def _cparams(n_grid, vmem=None):
    return pltpu.CompilerParams(
        dimension_semantics=("arbitrary",) * n_grid,
        vmem_limit_bytes=vmem,
    )


def _dot(a, b):
    return jnp.dot(a, b, preferred_element_type=F32)


def _dot_nt(a, b):
    return lax.dot_general(a, b, (((1,), (1,)), ((), ())), preferred_element_type=F32)


def _split_dot(a, b_bf16, terms):
    out = None
    rem = a
    for i in range(terms):
        part = rem.astype(BF16)
        d = _dot(part, b_bf16)
        out = d if out is None else out + d
        if i + 1 < terms:
            rem = rem - part.astype(F32)
    return out


def _log_sigmoid(x):
    return jnp.minimum(x, 0.0) - jnp.log1p(jnp.exp(-jnp.abs(x)))


def _sigmoid(x):
    return 1.0 / (1.0 + jnp.exp(-x))


def _in_proj_kernel(x_ref, g_ref, wa_ref, wf_ref, wb_ref, wg_ref, bf_ref, qn_ref, kn_ref, bd_ref,
                    fq_ref, fk_ref, fv_ref, lf_ref, sq_ref, sk_ref, sv_ref, ga_ref, gb_ref):
    x = x_ref[...]
    ms = jnp.mean(x * x, axis=-1, keepdims=True)
    n = (x * lax.rsqrt(ms + NORM_EPS) * g_ref[...]).astype(BF16)

    a = _dot(n, wa_ref[...])
    bd = bd_ref[...]

    def head_norm(t, gain):
        msq = _split_dot(t * t, bd, 2) * (1.0 / HEAD_DIM)
        return t * lax.rsqrt(msq + NORM_EPS) * gain

    fq_ref[...] = head_norm(a[:, :WIDTH], qn_ref[...])
    fk_ref[...] = head_norm(a[:, WIDTH:2 * WIDTH], kn_ref[...])
    fv_ref[...] = a[:, 2 * WIDTH:]

    f = _dot(n, wf_ref[...]) + bf_ref[...]
    lf_ref[...] = _log_sigmoid(f)[:, :N_HEADS]

    b = _dot(n, wb_ref[...])
    sq_ref[...] = b[:, :WIDTH]
    sk_ref[...] = b[:, WIDTH:2 * WIDTH]
    sv_ref[...] = b[:, 2 * WIDTH:]

    g = _dot(n, wg_ref[...])
    ga_ref[...] = _sigmoid(g[:, :D_MODEL])
    gb_ref[...] = _sigmoid(g[:, D_MODEL:])


def _in_proj(x, p):
    t = x.shape[0]
    tl = min(TOKEN_BLOCK, t)
    assert t % tl == 0
    row = lambda w: pl.BlockSpec((tl, w), lambda i: (i, 0))
    full = lambda a: pl.BlockSpec(a.shape, lambda i: (0,) * a.ndim)
    consts = (p["g_mix"], p["w_a"], p["w_f"], p["w_b"], p["w_g"], p["b_f"], p["qn"], p["kn"], p["bd"])
    widths = (WIDTH, WIDTH, WIDTH, N_HEADS, WIDTH, WIDTH, WIDTH, D_MODEL, D_MODEL)
    return pl.pallas_call(
        _in_proj_kernel,
        out_shape=tuple(jax.ShapeDtypeStruct((t, w), F32) for w in widths),
        grid=(t // tl,),
        in_specs=[row(D_MODEL)] + [full(c) for c in consts],
        out_specs=tuple(row(w) for w in widths),
        compiler_params=_cparams(1, VMEM_LIMIT),
        name="in_proj",
    )(x, *consts)


def _cumsum_kernel(x_ref, o_ref):
    n_chunks = x_ref.shape[1] // LANES
    r = lax.broadcasted_iota(jnp.int32, (LANES, LANES), 0)
    c = lax.broadcasted_iota(jnp.int32, (LANES, LANES), 1)
    upper = jnp.where(r <= c, 1.0, 0.0).astype(BF16)
    carry = jnp.zeros((x_ref.shape[0], 1), F32)
    for k in range(n_chunks):
        y = _split_dot(x_ref[:, k * LANES:(k + 1) * LANES], upper, 3) + carry
        o_ref[:, k * LANES:(k + 1) * LANES] = y
        carry = y[:, LANES - 1:LANES]


def _logf_cumsum(lf_rows):
    return pl.pallas_call(
        _cumsum_kernel,
        out_shape=jax.ShapeDtypeStruct(lf_rows.shape, F32),
        name="logf_cumsum",
    )(lf_rows)


def _head_masks():
    lane = lax.broadcasted_iota(jnp.int32, (1, LANES), 1)
    return lane < HEAD_DIM, lane >= HEAD_DIM


def _stack_heads(x):
    m0, m1 = _head_masks()
    return jnp.concatenate([jnp.where(m0, x, 0.0), jnp.where(m1, x, 0.0)], axis=0)


def _unstack_heads(y, tq):
    m0, _ = _head_masks()
    return jnp.where(m0, y[:tq], y[tq:])


def _stacked_pos(tq, tk):
    r = lax.broadcasted_iota(jnp.int32, (2 * tq, tk), 0)
    c = lax.broadcasted_iota(jnp.int32, (2 * tq, tk), 1)
    return jnp.where(r >= tq, r - tq, r), c


def _static_chunks(n, size):
    out = [(s, size) for s in range(0, n - size + 1, size)]
    done = len(out) * size
    if done < n:
        out.append((done, n - done))
    return out


def _tri_strict(n):
    r = lax.broadcasted_iota(jnp.int32, (n, n), 0)
    c = lax.broadcasted_iota(jnp.int32, (n, n), 1)
    return jnp.where(r > c, 1.0, 0.0).astype(BF16)


def _softmax_step(carry, qs, cq, is_head1, segments):
    scores = []
    for kb, _, ck, valid in segments:
        bias = jnp.where(is_head1, ck[1:2, :], ck[0:1, :])
        s = _dot_nt(qs, kb.astype(BF16)) + (cq - bias)
        if valid is not None:
            s = jnp.where(valid, s, MASK_VALUE)
        scores.append(s)
    m_new = functools.reduce(jnp.maximum, [jnp.max(s, axis=-1, keepdims=True) for s in scores])
    if carry is not None:
        m_old, l_old, acc_old = carry
        m_new = jnp.maximum(m_old, m_new)
    l = None
    acc = None
    for s, (_, vb, _, _) in zip(scores, segments):
        pr = jnp.exp(s - m_new)
        ls = jnp.sum(pr, axis=-1, keepdims=True)
        pv = _dot(pr.astype(BF16), vb.astype(BF16))
        l = ls if l is None else l + ls
        acc = pv if acc is None else acc + pv
    if carry is not None:
        alpha = jnp.exp(m_old - m_new)
        l = alpha * l_old + l
        acc = alpha * acc_old + acc
    return m_new, l, acc


def _fox_kernel(q_ref, kn_ref, vn_ref, kp_ref, vp_ref, cq_ref, cn_ref, cp_ref, o_ref, *, tq, p_len):
    lq = q_ref.shape[1]
    nq = lq // tq
    row, col = _stacked_pos(tq, tq)
    causal = col <= row
    is_head1 = lax.broadcasted_iota(jnp.int32, (2 * tq, 1), 0) >= tq
    past_size = p_len if nq > 1 else PAST_CHUNK_FOX
    past = [(kp_ref[0, s:s + n, :], vp_ref[0, s:s + n, :], cp_ref[0, 0, :, s:s + n], None)
            for s, n in _static_chunks(p_len, past_size)]

    def q_block(i):
        q0 = i * tq if nq == 1 else pl.multiple_of(i * tq, tq)
        qs = _stack_heads(q_ref[0, pl.ds(q0, tq), :] * ATTN_SCALE).astype(BF16)
        cqb = cq_ref[0, 0, pl.ds(q0, tq), :]
        cq = jnp.concatenate([cqb[:, 0:1], cqb[:, 1:2]], axis=0)
        diag = (kn_ref[0, pl.ds(q0, tq), :], vn_ref[0, pl.ds(q0, tq), :], cn_ref[0, 0, i], causal)
        carry = _softmax_step(None, qs, cq, is_head1, past + [diag])

        def new_body(j, carry):
            k0 = pl.multiple_of(j * tq, tq)
            seg = (kn_ref[0, pl.ds(k0, tq), :], vn_ref[0, pl.ds(k0, tq), :], cn_ref[0, 0, j], None)
            return _softmax_step(carry, qs, cq, is_head1, [seg])

        if nq > 1:
            carry = lax.fori_loop(0, i, new_body, carry)
        _, l, acc = carry
        o_ref[0, pl.ds(q0, tq), :] = _unstack_heads(acc / l, tq)

    if nq == 1:
        q_block(0)
    else:
        lax.fori_loop(0, nq, lambda i, _: (q_block(i), 0)[1], 0)


def _sb_kernel(q_ref, kn_ref, vn_ref, kp_ref, vp_ref, o_ref, *, tq, p_len):
    lq = q_ref.shape[1]
    nq = lq // tq
    row, col = _stacked_pos(tq, tq)
    strictly_before = col < row
    tri_q = _tri_strict(tq)

    def local(qs, kb, tri_m, valid=None):
        z = _dot_nt(qs, kb.astype(BF16))
        m = _log_sigmoid(-z)
        if valid is not None:
            m = jnp.where(valid, m, 0.0)
        tail_in = _split_dot(m, tri_m, 2)
        return z, m, tail_in, tail_in[:, 0:1] + m[:, 0:1]

    def weights(z, m, tail_in, run, valid=None):
        a = jnp.exp(z + m + tail_in) if run is None else jnp.exp(z + m + tail_in + run)
        if valid is not None:
            a = jnp.where(valid, a, 0.0)
        return a.astype(BF16)

    def q_block(i):
        q0 = i * tq if nq == 1 else pl.multiple_of(i * tq, tq)
        qs = _stack_heads(q_ref[0, pl.ds(q0, tq), :] * ATTN_SCALE).astype(BF16)
        z, m, tail_in, tot = local(qs, kn_ref[0, pl.ds(q0, tq), :], tri_q, strictly_before)
        acc = _dot(weights(z, m, tail_in, None, strictly_before), vn_ref[0, pl.ds(q0, tq), :].astype(BF16))
        run = tot

        if nq > 1:
            assert p_len <= ATTN_BLOCK
            zp, mp, tp, _ = local(qs, kp_ref[0], _tri_strict(p_len))
            past_acc = _dot(weights(zp, mp, tp, None), vp_ref[0].astype(BF16))

            def new_body(jj, carry):
                run, acc = carry
                k0 = pl.multiple_of((i - 1 - jj) * tq, tq)
                z, m, tail_in, tot = local(qs, kn_ref[0, pl.ds(k0, tq), :], tri_q)
                acc = acc + _dot(weights(z, m, tail_in, run), vn_ref[0, pl.ds(k0, tq), :].astype(BF16))
                return run + tot, acc

            run, acc = lax.fori_loop(0, i, new_body, (run, acc))
            acc = acc + jnp.exp(run) * past_acc
        else:
            chunks = _static_chunks(p_len, PAST_CHUNK_SB)
            tris = {n: _tri_strict(n) for n in {n for _, n in chunks}}
            for s, n in reversed(chunks):
                z, m, tail_in, tot = local(qs, kp_ref[0, s:s + n, :], tris[n])
                acc = acc + _dot(weights(z, m, tail_in, run), vp_ref[0, s:s + n, :].astype(BF16))
                run = run + tot
        o_ref[0, pl.ds(q0, tq), :] = _unstack_heads(acc, tq)

    if nq == 1:
        q_block(0)
    else:
        lax.fori_loop(0, nq, lambda i, _: (q_block(i), 0)[1], 0)


def _attn_specs(bsz, lq, p_len, past_shared):
    pair_new = pl.BlockSpec((1, lq, LANES), lambda b, p: (b, 0, p))
    if past_shared:
        pair_past = pl.BlockSpec((1, p_len, LANES), lambda b, p: (0, 0, p))
    else:
        pair_past = pl.BlockSpec((1, p_len, LANES), lambda b, p: (b, 0, p))
    return pair_new, pair_past


def _fox_attn(q, k_new, v_new, k_past, v_past, cq, cn, cp):
    bsz, lq, _ = q.shape
    p_len = k_past.shape[1]
    tq = min(ATTN_BLOCK, lq)
    pair_new, pair_past = _attn_specs(bsz, lq, p_len, k_past.shape[0] == 1)
    c_spec = lambda a: pl.BlockSpec((1, 1) + a.shape[2:], lambda b, p: (b, p) + (0,) * (a.ndim - 2))
    return pl.pallas_call(
        functools.partial(_fox_kernel, tq=tq, p_len=p_len),
        out_shape=jax.ShapeDtypeStruct(q.shape, F32),
        grid=(bsz, N_PAIRS),
        in_specs=[pair_new, pair_new, pair_new, pair_past, pair_past, c_spec(cq), c_spec(cn), c_spec(cp)],
        out_specs=pair_new,
        compiler_params=_cparams(2, VMEM_LIMIT),
        name="fox_attn",
    )(q, k_new, v_new, k_past, v_past, cq, cn, cp)


def _sb_attn(q, k_new, v_new, k_past, v_past):
    bsz, lq, _ = q.shape
    p_len = k_past.shape[1]
    tq = min(ATTN_BLOCK, lq)
    pair_new, pair_past = _attn_specs(bsz, lq, p_len, k_past.shape[0] == 1)
    return pl.pallas_call(
        functools.partial(_sb_kernel, tq=tq, p_len=p_len),
        out_shape=jax.ShapeDtypeStruct(q.shape, F32),
        grid=(bsz, N_PAIRS),
        in_specs=[pair_new, pair_new, pair_new, pair_past, pair_past],
        out_specs=pair_new,
        compiler_params=_cparams(2, VMEM_LIMIT),
        name="sb_attn",
    )(q, k_new, v_new, k_past, v_past)


def _topk_rows(vals, payload, k):
    n = vals.shape[0]
    rio = lax.broadcasted_iota(jnp.int32, vals.shape, 0)
    out_v, out_p = [], []
    for _ in range(k):
        mx = jnp.max(vals, axis=0, keepdims=True)
        pos = jnp.min(jnp.where(vals == mx, rio, n), axis=0, keepdims=True)
        sel = rio == pos
        out_v.append(mx)
        if payload is None:
            out_p.append(pos)
        else:
            out_p.append(jnp.sum(jnp.where(sel, payload, 0), axis=0, keepdims=True))
        vals = jnp.where(sel, -jnp.inf, vals)
    return jnp.concatenate(out_v, axis=0), jnp.concatenate(out_p, axis=0)


def _pair_rows(first, second):
    half = PEER_TOPK // 2
    rows = [first[0:1, :] + second]
    rows += [first[a:a + 1, :] + second[:half] for a in range(1, half)]
    rows.append(first[half:, :] + second[0:1, :])
    return jnp.concatenate(rows, axis=0)


def _mix_route_kernel(x_ref, of_ref, os_ref, ga_ref, gb_ref, wof_ref, wos_ref, wout_ref, gf_ref,
                      wq_ref, k1_ref, k2_ref, h_ref, n2_ref, idx_ref, gate_ref):
    yf = _dot(of_ref[...].astype(BF16), wof_ref[...])
    ys = _dot(os_ref[...].astype(BF16), wos_ref[...])
    merged = ga_ref[...] * yf + gb_ref[...] * ys
    h = x_ref[...] + _dot(merged.astype(BF16), wout_ref[...])
    h_ref[...] = h
    ms = jnp.mean(h * h, axis=-1, keepdims=True)
    n2 = h * lax.rsqrt(ms + NORM_EPS) * gf_ref[...]
    n2_ref[...] = n2
    qp = _dot(n2.astype(BF16), wq_ref[...]).astype(BF16)
    k1 = k1_ref[...]
    k2 = k2_ref[...]
    tl = x_ref.shape[0]
    for c0 in range(0, tl, LANES):
        rows = []
        for hd in range(PEER_HEADS):
            base = hd * 2 * PEER_HALF
            q1 = qp[c0:c0 + LANES, base:base + PEER_HALF]
            q2 = qp[c0:c0 + LANES, base + PEER_HALF:base + 2 * PEER_HALF]
            s1 = _dot_nt(k1, q1)
            s2 = _dot_nt(k2, q2)
            v1, i1 = _topk_rows(s1, None, PEER_TOPK)
            v2, i2 = _topk_rows(s2, None, PEER_TOPK)
            sc, e = _topk_rows(_pair_rows(v1, v2), _pair_rows(i1 * PEER_N_KEYS, i2), PEER_TOPK)
            ex = jnp.exp(sc - jnp.max(sc, axis=0, keepdims=True))
            gate = ex / jnp.sum(ex, axis=0, keepdims=True)
            r0 = hd * PEER_TOPK
            rows.append(e * ROWS_PER_EXPERT)
            gate_ref[r0:r0 + PEER_TOPK, c0:c0 + LANES] = gate
        idx_ref[c0:c0 + LANES, :] = jnp.concatenate(rows, axis=0).T


def _mix_route(x, o_fox, o_sb, ga, gb, p):
    t = x.shape[0]
    tl = min(TOKEN_BLOCK, t)
    assert t % tl == 0 and tl % LANES == 0
    row = lambda w: pl.BlockSpec((tl, w), lambda i: (i, 0))
    full = lambda a: pl.BlockSpec(a.shape, lambda i: (0,) * a.ndim)
    colblk = pl.BlockSpec((PEER_PICKS, tl), lambda i: (0, i))
    consts_a = (p["w_o_fox"], p["w_o_sb"], p["w_out"], p["g_ffn"], p["w_peer_q"], p["keys1"], p["keys2"])
    return pl.pallas_call(
        _mix_route_kernel,
        out_shape=(jax.ShapeDtypeStruct((t, D_MODEL), F32), jax.ShapeDtypeStruct((t, D_MODEL), F32),
                   jax.ShapeDtypeStruct((t, PEER_PICKS), jnp.int32), jax.ShapeDtypeStruct((PEER_PICKS, t), F32)),
        grid=(t // tl,),
        in_specs=[row(D_MODEL), row(WIDTH), row(WIDTH), row(D_MODEL), row(D_MODEL)] + [full(c) for c in consts_a],
        out_specs=(row(D_MODEL), row(D_MODEL), row(PEER_PICKS), colblk),
        compiler_params=_cparams(1, VMEM_LIMIT),
        name="mix_route",
    )(x, o_fox, o_sb, ga, gb, *consts_a)


def _unpack_pair(words):
    lo = pltpu.bitcast(words << 16, F32)
    hi = pltpu.bitcast(words & jnp.int32(-65536), F32)
    return lo, hi


def _route_copy(src_hbm, slots, sems, step, chunk, slot):
    chunks_per_step = PEER_TOKEN_BLOCK // ROUTE_CHUNK
    row0 = pl.multiple_of((step * chunks_per_step + chunk) * ROUTE_CHUNK, ROUTE_CHUNK)
    return pltpu.make_async_copy(src_hbm.at[pl.ds(row0, ROUTE_CHUNK)], slots.at[slot], sems.at[slot])


def _for_each_token(route_hbm, slots, sems, token_fn, carry):
    step = pl.program_id(0)
    n_steps = pl.num_programs(0)
    n_chunks = PEER_TOKEN_BLOCK // ROUTE_CHUNK

    @pl.when(step == 0)
    def _():
        _route_copy(route_hbm, slots, sems, step, 0, 0).start()

    def chunk_pair(cc, carry):
        for slot in range(2):
            chunk = 2 * cc + slot
            nxt = chunk + 1

            @pl.when(nxt < n_chunks)
            def _():
                _route_copy(route_hbm, slots, sems, step, nxt, 1 - slot).start()

            @pl.when((nxt == n_chunks) & (step + 1 < n_steps))
            def _():
                _route_copy(route_hbm, slots, sems, step + 1, 0, 1 - slot).start()

            _route_copy(route_hbm, slots, sems, step, chunk, slot).wait()
            for u in range(ROUTE_CHUNK):
                carry = token_fn(chunk * ROUTE_CHUNK + u, slot, u, carry)
        return carry

    return lax.fori_loop(0, n_chunks // 2, chunk_pair, carry)


def _gather_pair(tbl_ref, row_a, row_b):
    ra = pl.multiple_of(row_a, ROWS_PER_EXPERT)
    rb = pl.multiple_of(row_b, ROWS_PER_EXPERT)
    words = jnp.concatenate([tbl_ref[pl.ds(ra, ROWS_PER_EXPERT), :], tbl_ref[pl.ds(rb, ROWS_PER_EXPERT), :]], axis=0)
    return _unpack_pair(words)


def _peer_act_kernel(idx_hbm, idx_ref, x_ref, gate_ref, tbl_ref, pk_ref, slots, sems, prod_a, prod_b):
    tb = x_ref.shape[0]
    lane = lax.broadcasted_iota(jnp.int32, (PEER_PICKS, tb), 1)
    prods = (prod_a, prod_b)
    prod_b[...] = jnp.zeros_like(prod_b)

    def reduce(t, prod_ref, dots):
        part = prod_ref[pl.ds(0, PEER_PICKS, stride=ROWS_PER_EXPERT), :]
        for r in range(1, ROWS_PER_EXPERT):
            part = part + prod_ref[pl.ds(r, PEER_PICKS, stride=ROWS_PER_EXPERT), :]
        d = jnp.sum(part, axis=-1, keepdims=True)
        return jnp.where(lane == t, d, dots)

    def token(t, slot, u, dots):
        xt = x_ref[t]
        x_lo = jnp.concatenate([xt[:ROWS_PER_EXPERT]] * 2, axis=0)
        x_hi = jnp.concatenate([xt[ROWS_PER_EXPERT:]] * 2, axis=0)
        prod_ref = prods[u % 2]
        for j in range(0, PEER_PICKS, 2):
            lo, hi = _gather_pair(tbl_ref, slots[slot, u, j], slots[slot, u, j + 1])
            prod_ref[j * ROWS_PER_EXPERT:(j + 2) * ROWS_PER_EXPERT, :] = lo * x_lo + hi * x_hi
        return reduce(t - 1, prods[1 - u % 2], dots)

    dots = _for_each_token(idx_hbm, slots, sems, token, jnp.zeros((PEER_PICKS, tb), F32))
    dots = reduce(tb - 1, prods[(ROUTE_CHUNK - 1) % 2], dots)
    act = 0.5 * dots * (1.0 + lax.erf(dots * (2.0 ** -0.5)))
    w = (gate_ref[...] * act).T.astype(BF16).astype(F32)
    pk_ref[...] = pltpu.bitcast(w, jnp.int32) | idx_ref[...]


def _peer_out_kernel(pk_hbm, h_ref, tbl_ref, o_ref, slots, sems):
    n_acc = 2
    upper = lax.broadcasted_iota(jnp.int32, (SUBLANES, LANES), 0) >= ROWS_PER_EXPERT

    def token(t, slot, u, carry):
        acc_lo = [jnp.zeros((SUBLANES, LANES), F32) for _ in range(n_acc)]
        acc_hi = [jnp.zeros((SUBLANES, LANES), F32) for _ in range(n_acc)]
        for j in range(0, PEER_PICKS, 2):
            wa = slots[slot, u, j]
            wb = slots[slot, u, j + 1]
            lo, hi = _gather_pair(tbl_ref, wa & 0xFFFF, wb & 0xFFFF)
            wv = pltpu.bitcast(jnp.where(upper, wb, wa) & jnp.int32(-65536), F32)
            k = (j // 2) % n_acc
            acc_lo[k] = acc_lo[k] + wv * lo
            acc_hi[k] = acc_hi[k] + wv * hi
        lo = acc_lo[0] + acc_lo[1]
        hi = acc_hi[0] + acc_hi[1]
        peer = jnp.concatenate([lo[:ROWS_PER_EXPERT] + lo[ROWS_PER_EXPERT:],
                                hi[:ROWS_PER_EXPERT] + hi[ROWS_PER_EXPERT:]], axis=0)
        o_ref[t] = h_ref[t] + peer
        return carry

    _for_each_token(pk_hbm, slots, sems, token, 0)


def _peer_specs(t):
    tb = PEER_TOKEN_BLOCK
    assert t % tb == 0 and (tb // ROUTE_CHUNK) % 2 == 0
    hbm = pl.BlockSpec(memory_space=pl.ANY)
    vcol = pl.BlockSpec((PEER_PICKS, tb), lambda i: (0, i))
    vrow = pl.BlockSpec((tb, PEER_PICKS), lambda i: (i, 0))
    tok = pl.BlockSpec((tb, SUBLANES, LANES), lambda i: (i, 0, 0))
    route_scratch = [pltpu.SMEM((2, ROUTE_CHUNK, PEER_PICKS), jnp.int32), pltpu.SemaphoreType.DMA((2,))]
    return tb, hbm, vcol, vrow, tok, route_scratch


def _table_spec(tbl):
    return pl.BlockSpec(tbl.shape, lambda i: (0, 0), pipeline_mode=pl.Buffered(1))


def _peer_act(idx, n2_r, gate_t, tbl_u):
    t = n2_r.shape[0]
    tb, hbm, vcol, vrow, tok, route_scratch = _peer_specs(t)
    prod = pltpu.VMEM((PEER_PICKS * ROWS_PER_EXPERT, LANES), F32)
    return pl.pallas_call(
        _peer_act_kernel,
        out_shape=jax.ShapeDtypeStruct((t, PEER_PICKS), jnp.int32),
        grid=(t // tb,),
        in_specs=[hbm, vrow, tok, vcol, _table_spec(tbl_u)],
        out_specs=vrow,
        scratch_shapes=route_scratch + [prod, prod],
        compiler_params=_cparams(1, VMEM_LIMIT),
        name="peer_act",
    )(idx, idx, n2_r, gate_t, tbl_u)


def _peer_out(pk, h_r, tbl_v):
    t = h_r.shape[0]
    tb, hbm, vcol, vrow, tok, route_scratch = _peer_specs(t)
    return pl.pallas_call(
        _peer_out_kernel,
        out_shape=jax.ShapeDtypeStruct(h_r.shape, F32),
        grid=(t // tb,),
        in_specs=[hbm, tok, _table_spec(tbl_v)],
        out_specs=tok,
        scratch_shapes=route_scratch,
        compiler_params=_cparams(1, VMEM_LIMIT),
        name="peer_out",
    )(pk, h_r, tbl_v)


def _pack_table(tbl):
    bits = lax.bitcast_convert_type(tbl.astype(BF16), jnp.uint16).astype(jnp.uint32)
    half = D_MODEL // 2
    packed = bits[:, :half] | (bits[:, half:] << 16)
    return lax.bitcast_convert_type(packed, jnp.int32).reshape(-1, LANES)


def _prep_params(norm_mix, w_in, b_forget, fox_q_norm, fox_k_norm, w_o_fox, w_o_sb, w_out,
                 norm_ffn, w_peer_q, keys1, keys2, expert_u, expert_v):
    w = w_in.astype(BF16)
    o_f = 3 * WIDTH
    o_b = o_f + N_HEADS
    o_g = o_b + 3 * WIDTH
    pad = LANES - N_HEADS
    head = lax.broadcasted_iota(jnp.int32, (WIDTH, WIDTH), 0) // HEAD_DIM
    head_t = lax.broadcasted_iota(jnp.int32, (WIDTH, WIDTH), 1) // HEAD_DIM
    return {
        "g_mix": norm_mix.reshape(1, D_MODEL),
        "w_a": w[:, :o_f],
        "w_f": jnp.pad(w[:, o_f:o_b], ((0, 0), (0, pad))),
        "w_b": w[:, o_b:o_g],
        "w_g": w[:, o_g:],
        "b_f": jnp.pad(b_forget.reshape(1, N_HEADS), ((0, 0), (0, pad))),
        "qn": jnp.tile(fox_q_norm.reshape(1, HEAD_DIM), (1, N_HEADS)),
        "kn": jnp.tile(fox_k_norm.reshape(1, HEAD_DIM), (1, N_HEADS)),
        "bd": (head == head_t).astype(BF16),
        "w_o_fox": w_o_fox.astype(BF16),
        "w_o_sb": w_o_sb.astype(BF16),
        "w_out": w_out.astype(BF16),
        "g_ffn": norm_ffn.reshape(1, D_MODEL),
        "w_peer_q": w_peer_q.astype(BF16),
        "keys1": keys1.astype(BF16),
        "keys2": keys2.astype(BF16),
        "tbl_u": _pack_table(expert_u),
        "tbl_v": _pack_table(expert_v),
    }


def _forget_bias_layouts(lf_past, lf_new, tq):
    bsz, lq, _ = lf_new.shape
    p_len = lf_past.shape[1]
    total = p_len + lq
    lp = -(-total // LANES) * LANES
    lf_all = jnp.concatenate([jnp.broadcast_to(lf_past, (bsz, p_len, N_HEADS)), lf_new], axis=1)
    rows = jnp.pad(jnp.transpose(lf_all, (0, 2, 1)), ((0, 0), (0, 0), (0, lp - total)))
    c = _logf_cumsum(rows.reshape(bsz * N_HEADS, lp)).reshape(bsz, N_PAIRS, 2, lp)
    c_new = c[..., p_len:total]
    cq = jnp.transpose(c_new, (0, 1, 3, 2))
    cn = jnp.transpose(c_new.reshape(bsz, N_PAIRS, 2, lq // tq, tq), (0, 1, 3, 2, 4))
    cp = c[..., :p_len]
    return cq, cn, cp


def _group_forward(x, past, p):
    bsz, lq, _ = x.shape
    t = bsz * lq
    x2 = x.reshape(t, D_MODEL)
    fq, fk, fv, lf, sq, sk, sv, ga, gb = _in_proj(x2, p)
    b3 = lambda a: a.reshape(bsz, lq, a.shape[-1])
    pk, pv, plf, psk, psv = past
    tq = min(ATTN_BLOCK, lq)
    cq, cn, cp = _forget_bias_layouts(plf, b3(lf), tq)
    o_fox = _fox_attn(b3(fq), b3(fk), b3(fv), pk, pv, cq, cn, cp)
    o_sb = _sb_attn(b3(sq), b3(sk), b3(sv), psk, psv)
    h, n2, idx, gate_t = _mix_route(x2, o_fox.reshape(t, WIDTH), o_sb.reshape(t, WIDTH), ga, gb, p)
    tile = lambda a: a.reshape(t, SUBLANES, LANES)
    routes = _peer_act(idx, tile(n2), gate_t, p["tbl_u"])
    out = _peer_out(routes, tile(h), p["tbl_v"])
    return out.reshape(bsz, lq, D_MODEL), (b3(fk), b3(fv), b3(lf), b3(sk), b3(sv))


def kernel(x_prompt, x_sample, cache_fox_k, cache_fox_v, cache_fox_logf, cache_sb_k, cache_sb_v,
           meta_tokens, norm_mix, w_in, b_forget, fox_q_norm, fox_k_norm, w_o_fox, w_o_sb, w_out,
           norm_ffn, w_peer_q, peer_sub_keys_1, peer_sub_keys_2, expert_u, expert_v):
    assert norm_mix.shape[0] == 1, "single-layer stack"
    p = _prep_params(norm_mix[0], w_in[0], b_forget[0], fox_q_norm[0], fox_k_norm[0], w_o_fox[0],
                     w_o_sb[0], w_out[0], norm_ffn[0], w_peer_q[0], peer_sub_keys_1[0],
                     peer_sub_keys_2[0], expert_u[0], expert_v[0])
    bsz = x_prompt.shape[0]
    n_meta = meta_tokens.shape[0]

    _, mk, mv, mlf, _, msk, msv, _, _ = _in_proj(meta_tokens.astype(x_prompt.dtype), p)
    meta_past = (mk[None], mv[None], mlf[None], msk[None], msv[None])
    y_prompt, rows_p = _group_forward(x_prompt, meta_past, p)

    def with_meta(meta_rows, new_rows, tail_shape):
        full = jnp.concatenate([jnp.broadcast_to(meta_rows[None], (bsz,) + meta_rows.shape), new_rows], axis=1)
        return full.reshape((1, bsz, n_meta + new_rows.shape[1]) + tail_shape)

    hd = (N_HEADS, HEAD_DIM)
    out_p = (with_meta(mk, rows_p[0], hd), with_meta(mv, rows_p[1], hd), with_meta(mlf, rows_p[2], (N_HEADS,)),
             with_meta(msk, rows_p[3], hd), with_meta(msv, rows_p[4], hd))

    dbsz, plen = cache_fox_k.shape[1], cache_fox_k.shape[2]
    flat = lambda c: c[0].reshape(dbsz, plen, -1)
    sample_past = (flat(cache_fox_k), flat(cache_fox_v), flat(cache_fox_logf), flat(cache_sb_k), flat(cache_sb_v))
    y_sample, rows_s = _group_forward(x_sample, sample_past, p)
    lq = x_sample.shape[1]
    out_s = (rows_s[0].reshape(1, dbsz, lq, *hd), rows_s[1].reshape(1, dbsz, lq, *hd),
             rows_s[2].reshape(1, dbsz, lq, N_HEADS), rows_s[3].reshape(1, dbsz, lq, *hd),
             rows_s[4].reshape(1, dbsz, lq, *hd))
    return (y_prompt, y_sample) + out_p + out_s
```

```python
import functools

import jax
import jax.numpy as jnp
from jax import lax
from jax.experimental import pallas as pl
from jax.experimental.pallas import tpu as pltpu

F32 = jnp.float32
BF16 = jnp.bfloat16

D_MODEL = 1024
HEAD_DIM = 64
N_HEADS = 8
WIDTH = N_HEADS * HEAD_DIM
N_PAIRS = N_HEADS // 2
NORM_EPS = 1e-6
MASK_VALUE = -1e30
ATTN_SCALE = HEAD_DIM ** -0.5

PEER_HEADS = 8
PEER_N_KEYS = 128
PEER_HALF = 128
PEER_TOPK = 16
PEER_PICKS = PEER_HEADS * PEER_TOPK

LANES = 128
SUBLANES = 8
ROWS_PER_EXPERT = D_MODEL // 2 // LANES

TOKEN_BLOCK = 256
ATTN_BLOCK = 256
PAST_CHUNK_FOX = 512
PAST_CHUNK_SB = 128
PEER_TOKEN_BLOCK = 128
ROUTE_CHUNK = 4
ROUTE_SLOTS = 4
VMEM_LIMIT = 56 * 1024 * 1024


def _cparams(n_grid, vmem=None):
    return pltpu.CompilerParams(
        dimension_semantics=("arbitrary",) * n_grid,
        vmem_limit_bytes=vmem,
    )


def _dot(a, b):
    return jnp.dot(a, b, preferred_element_type=F32)


def _dot_nt(a, b):
    return lax.dot_general(a, b, (((1,), (1,)), ((), ())), preferred_element_type=F32)


def _split_dot(a, b_bf16, terms):
    out = None
    rem = a
    for i in range(terms):
        part = rem.astype(BF16)
        d = _dot(part, b_bf16)
        out = d if out is None else out + d
        if i + 1 < terms:
            rem = rem - part.astype(F32)
    return out


def _log_sigmoid(x):
    return jnp.minimum(x, 0.0) - jnp.log(1.0 + jnp.exp(-jnp.abs(x)))


def _sigmoid(x):
    return 1.0 / (1.0 + jnp.exp(-x))


def _in_proj_kernel(x_ref, g_ref, wa_ref, wf_ref, wb_ref, wg_ref, bf_ref, qn_ref, kn_ref, bd_ref,
                    fq_ref, fk_ref, fv_ref, lf_ref, sq_ref, sk_ref, sv_ref, ga_ref, gb_ref):
    x = x_ref[...]
    ms = jnp.mean(x * x, axis=-1, keepdims=True)
    n = (x * lax.rsqrt(ms + NORM_EPS) * g_ref[...]).astype(BF16)

    a = _dot(n, wa_ref[...])
    bd = bd_ref[...]

    def head_norm(t, gain):
        msq = _split_dot(t * t, bd, 2) * (1.0 / HEAD_DIM)
        return t * lax.rsqrt(msq + NORM_EPS) * gain

    fq_ref[...] = head_norm(a[:, :WIDTH], qn_ref[...])
    fk_ref[...] = head_norm(a[:, WIDTH:2 * WIDTH], kn_ref[...])
    fv_ref[...] = a[:, 2 * WIDTH:]

    f = _dot(n, wf_ref[...]) + bf_ref[...]
    lf_ref[...] = _log_sigmoid(f)[:, :N_HEADS]

    b = _dot(n, wb_ref[...])
    sq_ref[...] = b[:, :WIDTH]
    sk_ref[...] = b[:, WIDTH:2 * WIDTH]
    sv_ref[...] = b[:, 2 * WIDTH:]

    g = _dot(n, wg_ref[...])
    ga_ref[...] = _sigmoid(g[:, :D_MODEL])
    gb_ref[...] = _sigmoid(g[:, D_MODEL:])


def _in_proj(x, p):
    t = x.shape[0]
    tl = min(TOKEN_BLOCK, t)
    assert t % tl == 0
    row = lambda w: pl.BlockSpec((tl, w), lambda i: (i, 0))
    full = lambda a: pl.BlockSpec(a.shape, lambda i: (0,) * a.ndim)
    consts = (p["g_mix"], p["w_a"], p["w_f"], p["w_b"], p["w_g"], p["b_f"], p["qn"], p["kn"], p["bd"])
    widths = (WIDTH, WIDTH, WIDTH, N_HEADS, WIDTH, WIDTH, WIDTH, D_MODEL, D_MODEL)
    return pl.pallas_call(
        _in_proj_kernel,
        out_shape=tuple(jax.ShapeDtypeStruct((t, w), F32) for w in widths),
        grid=(t // tl,),
        in_specs=[row(D_MODEL)] + [full(c) for c in consts],
        out_specs=tuple(row(w) for w in widths),
        compiler_params=_cparams(1, VMEM_LIMIT),
        name="in_proj",
    )(x, *consts)


def _cumsum_kernel(x_ref, o_ref):
    n_chunks = x_ref.shape[1] // LANES
    r = lax.broadcasted_iota(jnp.int32, (LANES, LANES), 0)
    c = lax.broadcasted_iota(jnp.int32, (LANES, LANES), 1)
    upper = jnp.where(r <= c, 1.0, 0.0).astype(BF16)
    carry = jnp.zeros((x_ref.shape[0], 1), F32)
    for k in range(n_chunks):
        y = _split_dot(x_ref[:, k * LANES:(k + 1) * LANES], upper, 3) + carry
        o_ref[:, k * LANES:(k + 1) * LANES] = y
        carry = y[:, LANES - 1:LANES]


def _logf_cumsum(lf_rows):
    return pl.pallas_call(
        _cumsum_kernel,
        out_shape=jax.ShapeDtypeStruct(lf_rows.shape, F32),
        name="logf_cumsum",
    )(lf_rows)


def _head_masks():
    lane = lax.broadcasted_iota(jnp.int32, (1, LANES), 1)
    return lane < HEAD_DIM, lane >= HEAD_DIM


def _stack_heads(x):
    m0, m1 = _head_masks()
    return jnp.concatenate([jnp.where(m0, x, 0.0), jnp.where(m1, x, 0.0)], axis=0)


def _unstack_heads(y, tq):
    m0, _ = _head_masks()
    return jnp.where(m0, y[:tq], y[tq:])


def _stacked_pos(tq, tk):
    r = lax.broadcasted_iota(jnp.int32, (2 * tq, tk), 0)
    c = lax.broadcasted_iota(jnp.int32, (2 * tq, tk), 1)
    return jnp.where(r >= tq, r - tq, r), c


def _static_chunks(n, size):
    out = [(s, size) for s in range(0, n - size + 1, size)]
    done = len(out) * size
    if done < n:
        out.append((done, n - done))
    return out


def _tri_strict(n):
    r = lax.broadcasted_iota(jnp.int32, (n, n), 0)
    c = lax.broadcasted_iota(jnp.int32, (n, n), 1)
    return jnp.where(r > c, 1.0, 0.0).astype(BF16)


def _softmax_step(carry, qs, is_head1, segments):
    scores = []
    for kb, _, ck, valid in segments:
        bias = jnp.where(is_head1, ck[1:2, :], ck[0:1, :])
        s = _dot_nt(qs, kb.astype(BF16)) - bias
        if valid is not None:
            s = jnp.where(valid, s, MASK_VALUE)
        scores.append(s)
    m_new = functools.reduce(jnp.maximum, [jnp.max(s, axis=-1, keepdims=True) for s in scores])
    if carry is not None:
        m_old, l_old, acc_old = carry
        m_new = jnp.maximum(m_old, m_new)
    l = None
    acc = None
    for s, (_, vb, _, _) in zip(scores, segments):
        pr = jnp.exp(s - m_new)
        ls = jnp.sum(pr, axis=-1, keepdims=True)
        pv = _dot(pr.astype(BF16), vb.astype(BF16))
        l = ls if l is None else l + ls
        acc = pv if acc is None else acc + pv
    if carry is not None:
        alpha = jnp.exp(m_old - m_new)
        l = alpha * l_old + l
        acc = alpha * acc_old + acc
    return m_new, l, acc


def _fox_kernel(q_ref, kn_ref, vn_ref, kp_ref, vp_ref, cn_ref, cp_ref, o_ref, *, tq, p_len):
    lq = q_ref.shape[1]
    nq = lq // tq
    row, col = _stacked_pos(tq, tq)
    causal = col <= row
    is_head1 = lax.broadcasted_iota(jnp.int32, (2 * tq, 1), 0) >= tq
    past_size = p_len if nq > 1 else PAST_CHUNK_FOX
    past = [(kp_ref[0, s:s + n, :], vp_ref[0, s:s + n, :], cp_ref[0, 0, :, s:s + n], None)
            for s, n in _static_chunks(p_len, past_size)]

    def q_block(i):
        q0 = i * tq if nq == 1 else pl.multiple_of(i * tq, tq)
        qs = _stack_heads(q_ref[0, pl.ds(q0, tq), :] * ATTN_SCALE).astype(BF16)
        diag = (kn_ref[0, pl.ds(q0, tq), :], vn_ref[0, pl.ds(q0, tq), :], cn_ref[0, 0, i], causal)
        carry = _softmax_step(None, qs, is_head1, past + [diag])

        def new_body(j, carry):
            k0 = pl.multiple_of(j * tq, tq)
            seg = (kn_ref[0, pl.ds(k0, tq), :], vn_ref[0, pl.ds(k0, tq), :], cn_ref[0, 0, j], None)
            return _softmax_step(carry, qs, is_head1, [seg])

        if nq > 1:
            carry = lax.fori_loop(0, i, new_body, carry)
        _, l, acc = carry
        o_ref[0, pl.ds(q0, tq), :] = _unstack_heads(acc / l, tq)

    if nq == 1:
        q_block(0)
    else:
        lax.fori_loop(0, nq, lambda i, _: (q_block(i), 0)[1], 0)


def _sb_kernel(q_ref, kn_ref, vn_ref, kp_ref, vp_ref, o_ref, *, tq, p_len):
    lq = q_ref.shape[1]
    nq = lq // tq
    row, col = _stacked_pos(tq, tq)
    strictly_before = col < row
    tri_q = _tri_strict(tq)

    def local(qs, kb, tri_m, valid=None):
        z = _dot_nt(qs, kb.astype(BF16))
        m = _log_sigmoid(-z)
        if valid is not None:
            m = jnp.where(valid, m, 0.0)
        tail_in = _split_dot(m, tri_m, 2)
        return z, m, tail_in, tail_in[:, 0:1] + m[:, 0:1]

    def weights(z, m, tail_in, run, valid=None):
        a = jnp.exp(z + m + tail_in) if run is None else jnp.exp(z + m + tail_in + run)
        if valid is not None:
            a = jnp.where(valid, a, 0.0)
        return a.astype(BF16)

    def q_block(i):
        q0 = i * tq if nq == 1 else pl.multiple_of(i * tq, tq)
        qs = _stack_heads(q_ref[0, pl.ds(q0, tq), :] * ATTN_SCALE).astype(BF16)
        z, m, tail_in, tot = local(qs, kn_ref[0, pl.ds(q0, tq), :], tri_q, strictly_before)
        acc = _dot(weights(z, m, tail_in, None, strictly_before), vn_ref[0, pl.ds(q0, tq), :].astype(BF16))
        run = tot

        if nq > 1:
            assert p_len <= ATTN_BLOCK
            zp, mp, tp, _ = local(qs, kp_ref[0], _tri_strict(p_len))
            past_acc = _dot(weights(zp, mp, tp, None), vp_ref[0].astype(BF16))

            def new_body(jj, carry):
                run, acc = carry
                k0 = pl.multiple_of((i - 1 - jj) * tq, tq)
                z, m, tail_in, tot = local(qs, kn_ref[0, pl.ds(k0, tq), :], tri_q)
                acc = acc + _dot(weights(z, m, tail_in, run), vn_ref[0, pl.ds(k0, tq), :].astype(BF16))
                return run + tot, acc

            run, acc = lax.fori_loop(0, i, new_body, (run, acc))
            acc = acc + jnp.exp(run) * past_acc
        else:
            chunks = _static_chunks(p_len, PAST_CHUNK_SB)
            tris = {n: _tri_strict(n) for n in {n for _, n in chunks}}
            for s, n in reversed(chunks):
                z, m, tail_in, tot = local(qs, kp_ref[0, s:s + n, :], tris[n])
                acc = acc + _dot(weights(z, m, tail_in, run), vp_ref[0, s:s + n, :].astype(BF16))
                run = run + tot
        o_ref[0, pl.ds(q0, tq), :] = _unstack_heads(acc, tq)

    if nq == 1:
        q_block(0)
    else:
        lax.fori_loop(0, nq, lambda i, _: (q_block(i), 0)[1], 0)


def _attn_specs(bsz, lq, p_len, past_shared):
    pair_new = pl.BlockSpec((1, lq, LANES), lambda b, p: (b, 0, p))
    if past_shared:
        pair_past = pl.BlockSpec((1, p_len, LANES), lambda b, p: (0, 0, p))
    else:
        pair_past = pl.BlockSpec((1, p_len, LANES), lambda b, p: (b, 0, p))
    return pair_new, pair_past


def _fox_attn(q, k_new, v_new, k_past, v_past, cn, cp):
    bsz, lq, _ = q.shape
    p_len = k_past.shape[1]
    tq = min(ATTN_BLOCK, lq)
    pair_new, pair_past = _attn_specs(bsz, lq, p_len, k_past.shape[0] == 1)
    c_spec = lambda a: pl.BlockSpec((1, 1) + a.shape[2:], lambda b, p: (b, p) + (0,) * (a.ndim - 2))
    return pl.pallas_call(
        functools.partial(_fox_kernel, tq=tq, p_len=p_len),
        out_shape=jax.ShapeDtypeStruct(q.shape, F32),
        grid=(bsz, N_PAIRS),
        in_specs=[pair_new, pair_new, pair_new, pair_past, pair_past, c_spec(cn), c_spec(cp)],
        out_specs=pair_new,
        compiler_params=_cparams(2, VMEM_LIMIT),
        name="fox_attn",
    )(q, k_new, v_new, k_past, v_past, cn, cp)


def _sb_attn(q, k_new, v_new, k_past, v_past):
    bsz, lq, _ = q.shape
    p_len = k_past.shape[1]
    tq = min(ATTN_BLOCK, lq)
    pair_new, pair_past = _attn_specs(bsz, lq, p_len, k_past.shape[0] == 1)
    return pl.pallas_call(
        functools.partial(_sb_kernel, tq=tq, p_len=p_len),
        out_shape=jax.ShapeDtypeStruct(q.shape, F32),
        grid=(bsz, N_PAIRS),
        in_specs=[pair_new, pair_new, pair_new, pair_past, pair_past],
        out_specs=pair_new,
        compiler_params=_cparams(2, VMEM_LIMIT),
        name="sb_attn",
    )(q, k_new, v_new, k_past, v_past)


def _topk_rows(vals, payload, k):
    n = vals.shape[0]
    rio = lax.broadcasted_iota(jnp.int32, vals.shape, 0)
    out_v, out_p = [], []
    for _ in range(k):
        mx = jnp.max(vals, axis=0, keepdims=True)
        pos = jnp.min(jnp.where(vals == mx, rio, n), axis=0, keepdims=True)
        sel = rio == pos
        out_v.append(mx)
        if payload is None:
            out_p.append(pos)
        else:
            out_p.append(jnp.sum(jnp.where(sel, payload, 0), axis=0, keepdims=True))
        vals = jnp.where(sel, -jnp.inf, vals)
    return jnp.concatenate(out_v, axis=0), jnp.concatenate(out_p, axis=0)


def _pair_rows(first, second):
    half = PEER_TOPK // 2
    rows = [first[0:1, :] + second]
    rows += [first[a:a + 1, :] + second[:half] for a in range(1, half)]
    rows.append(first[half:, :] + second[0:1, :])
    return jnp.concatenate(rows, axis=0)


def _mix_route_kernel(x_ref, of_ref, os_ref, ga_ref, gb_ref, wof_ref, wos_ref, wout_ref, gf_ref,
                      wq_ref, k1_ref, k2_ref, h_ref, n2_ref, idx_ref, gate_ref):
    yf = _dot(of_ref[...].astype(BF16), wof_ref[...])
    ys = _dot(os_ref[...].astype(BF16), wos_ref[...])
    merged = ga_ref[...] * yf + gb_ref[...] * ys
    h = x_ref[...] + _dot(merged.astype(BF16), wout_ref[...])
    h_ref[...] = h
    ms = jnp.mean(h * h, axis=-1, keepdims=True)
    n2 = h * lax.rsqrt(ms + NORM_EPS) * gf_ref[...]
    n2_ref[...] = n2
    qp = _dot(n2.astype(BF16), wq_ref[...]).astype(BF16)
    k1 = k1_ref[...]
    k2 = k2_ref[...]
    tl = x_ref.shape[0]
    for c0 in range(0, tl, LANES):
        rows = []
        for hd in range(PEER_HEADS):
            base = hd * 2 * PEER_HALF
            q1 = qp[c0:c0 + LANES, base:base + PEER_HALF]
            q2 = qp[c0:c0 + LANES, base + PEER_HALF:base + 2 * PEER_HALF]
            s1 = _dot_nt(k1, q1)
            s2 = _dot_nt(k2, q2)
            v1, i1 = _topk_rows(s1, None, PEER_TOPK)
            v2, i2 = _topk_rows(s2, None, PEER_TOPK)
            sc, e = _topk_rows(_pair_rows(v1, v2), _pair_rows(i1 * PEER_N_KEYS, i2), PEER_TOPK)
            ex = jnp.exp(sc - jnp.max(sc, axis=0, keepdims=True))
            gate = ex / jnp.sum(ex, axis=0, keepdims=True)
            r0 = hd * PEER_TOPK
            rows.append(e * ROWS_PER_EXPERT)
            gate_ref[r0:r0 + PEER_TOPK, c0:c0 + LANES] = gate
        idx_ref[c0:c0 + LANES, :] = jnp.concatenate(rows, axis=0).T


def _mix_route(x, o_fox, o_sb, ga, gb, p):
    t = x.shape[0]
    tl = min(TOKEN_BLOCK, t)
    assert t % tl == 0 and tl % LANES == 0
    row = lambda w: pl.BlockSpec((tl, w), lambda i: (i, 0))
    full = lambda a: pl.BlockSpec(a.shape, lambda i: (0,) * a.ndim)
    colblk = pl.BlockSpec((PEER_PICKS, tl), lambda i: (0, i))
    consts_a = (p["w_o_fox"], p["w_o_sb"], p["w_out"], p["g_ffn"], p["w_peer_q"], p["keys1"], p["keys2"])
    return pl.pallas_call(
        _mix_route_kernel,
        out_shape=(jax.ShapeDtypeStruct((t, D_MODEL), F32), jax.ShapeDtypeStruct((t, D_MODEL), F32),
                   jax.ShapeDtypeStruct((t, PEER_PICKS), jnp.int32), jax.ShapeDtypeStruct((PEER_PICKS, t), F32)),
        grid=(t // tl,),
        in_specs=[row(D_MODEL), row(WIDTH), row(WIDTH), row(D_MODEL), row(D_MODEL)] + [full(c) for c in consts_a],
        out_specs=(row(D_MODEL), row(D_MODEL), row(PEER_PICKS), colblk),
        compiler_params=_cparams(1, VMEM_LIMIT),
        name="mix_route",
    )(x, o_fox, o_sb, ga, gb, *consts_a)


def _unpack_pair(words):
    lo = pltpu.bitcast(words << 16, F32)
    hi = pltpu.bitcast(words & jnp.int32(-65536), F32)
    return lo, hi


def _route_copy(src_hbm, slots, sems, chunk, slot):
    row0 = pl.multiple_of(chunk * ROUTE_CHUNK, ROUTE_CHUNK)
    return pltpu.make_async_copy(src_hbm.at[pl.ds(row0, ROUTE_CHUNK)], slots.at[slot], sems.at[slot])


def _for_each_token(route_hbm, slots, sems, token_fn, carry):
    step = pl.program_id(0)
    chunks_per_step = PEER_TOKEN_BLOCK // ROUTE_CHUNK
    total_chunks = pl.num_programs(0) * chunks_per_step
    ahead = ROUTE_SLOTS - 1

    @pl.when(step == 0)
    def _():
        for c in range(ahead):
            _route_copy(route_hbm, slots, sems, c, c).start()

    def slot_round(g, carry):
        for slot in range(ROUTE_SLOTS):
            local = g * ROUTE_SLOTS + slot
            chunk = step * chunks_per_step + local

            @pl.when(chunk + ahead < total_chunks)
            def _():
                _route_copy(route_hbm, slots, sems, chunk + ahead, (slot + ahead) % ROUTE_SLOTS).start()

            _route_copy(route_hbm, slots, sems, chunk, slot).wait()
            for u in range(ROUTE_CHUNK):
                carry = token_fn(local * ROUTE_CHUNK + u, slot, u, carry)
        return carry

    return lax.fori_loop(0, chunks_per_step // ROUTE_SLOTS, slot_round, carry)


def _gather_pair(tbl_ref, row_a, row_b):
    ra = pl.multiple_of(row_a, ROWS_PER_EXPERT)
    rb = pl.multiple_of(row_b, ROWS_PER_EXPERT)
    words = jnp.concatenate([tbl_ref[pl.ds(ra, ROWS_PER_EXPERT), :], tbl_ref[pl.ds(rb, ROWS_PER_EXPERT), :]], axis=0)
    return _unpack_pair(words)


def _peer_act_kernel(idx_hbm, idx_ref, x_ref, gate_ref, tbl_ref, pk_ref, slots, sems, prod_a, prod_b):
    tb = x_ref.shape[0]
    lane = lax.broadcasted_iota(jnp.int32, (PEER_PICKS, tb), 1)
    prods = (prod_a, prod_b)
    prod_b[...] = jnp.zeros_like(prod_b)

    def reduce(t, prod_ref, dots):
        part = prod_ref[pl.ds(0, PEER_PICKS, stride=ROWS_PER_EXPERT), :]
        for r in range(1, ROWS_PER_EXPERT):
            part = part + prod_ref[pl.ds(r, PEER_PICKS, stride=ROWS_PER_EXPERT), :]
        d = jnp.sum(part, axis=-1, keepdims=True)
        return jnp.where(lane == t, d, dots)

    def token(t, slot, u, dots):
        xt = x_ref[t]
        x_lo = jnp.concatenate([xt[:ROWS_PER_EXPERT]] * 2, axis=0)
        x_hi = jnp.concatenate([xt[ROWS_PER_EXPERT:]] * 2, axis=0)
        prod_ref = prods[u % 2]
        for j in range(0, PEER_PICKS, 2):
            lo, hi = _gather_pair(tbl_ref, slots[slot, u, j], slots[slot, u, j + 1])
            prod_ref[j * ROWS_PER_EXPERT:(j + 2) * ROWS_PER_EXPERT, :] = lo * x_lo + hi * x_hi
        return reduce(t - 1, prods[1 - u % 2], dots)

    dots = _for_each_token(idx_hbm, slots, sems, token, jnp.zeros((PEER_PICKS, tb), F32))
    dots = reduce(tb - 1, prods[(ROUTE_CHUNK - 1) % 2], dots)
    act = 0.5 * dots * (1.0 + lax.erf(dots * (2.0 ** -0.5)))
    w = (gate_ref[...] * act).T.astype(BF16).astype(F32)
    pk_ref[...] = pltpu.bitcast(w, jnp.int32) | idx_ref[...]


def _peer_out_kernel(pk_hbm, h_ref, tbl_ref, o_ref, slots, sems):
    n_acc = 2
    upper = lax.broadcasted_iota(jnp.int32, (SUBLANES, LANES), 0) >= ROWS_PER_EXPERT

    def token(t, slot, u, carry):
        acc_lo = [jnp.zeros((SUBLANES, LANES), F32) for _ in range(n_acc)]
        acc_hi = [jnp.zeros((SUBLANES, LANES), F32) for _ in range(n_acc)]
        for j in range(0, PEER_PICKS, 2):
            wa = slots[slot, u, j]
            wb = slots[slot, u, j + 1]
            lo, hi = _gather_pair(tbl_ref, wa & 0xFFFF, wb & 0xFFFF)
            wv = pltpu.bitcast(jnp.where(upper, wb, wa) & jnp.int32(-65536), F32)
            k = (j // 2) % n_acc
            acc_lo[k] = acc_lo[k] + wv * lo
            acc_hi[k] = acc_hi[k] + wv * hi
        lo = acc_lo[0] + acc_lo[1]
        hi = acc_hi[0] + acc_hi[1]
        peer = jnp.concatenate([lo[:ROWS_PER_EXPERT] + lo[ROWS_PER_EXPERT:],
                                hi[:ROWS_PER_EXPERT] + hi[ROWS_PER_EXPERT:]], axis=0)
        o_ref[t] = h_ref[t] + peer
        return carry

    _for_each_token(pk_hbm, slots, sems, token, 0)


def _peer_specs(t):
    tb = PEER_TOKEN_BLOCK
    assert t % tb == 0 and tb % (ROUTE_CHUNK * ROUTE_SLOTS) == 0 and ROUTE_CHUNK % 2 == 0
    hbm = pl.BlockSpec(memory_space=pl.ANY)
    vcol = pl.BlockSpec((PEER_PICKS, tb), lambda i: (0, i))
    vrow = pl.BlockSpec((tb, PEER_PICKS), lambda i: (i, 0))
    tok = pl.BlockSpec((tb, SUBLANES, LANES), lambda i: (i, 0, 0))
    route_scratch = [pltpu.SMEM((ROUTE_SLOTS, ROUTE_CHUNK, PEER_PICKS), jnp.int32),
                     pltpu.SemaphoreType.DMA((ROUTE_SLOTS,))]
    return tb, hbm, vcol, vrow, tok, route_scratch


def _table_spec(tbl):
    return pl.BlockSpec(tbl.shape, lambda i: (0, 0), pipeline_mode=pl.Buffered(1))


def _peer_act(idx, n2_r, gate_t, tbl_u):
    t = n2_r.shape[0]
    tb, hbm, vcol, vrow, tok, route_scratch = _peer_specs(t)
    prod = pltpu.VMEM((PEER_PICKS * ROWS_PER_EXPERT, LANES), F32)
    return pl.pallas_call(
        _peer_act_kernel,
        out_shape=jax.ShapeDtypeStruct((t, PEER_PICKS), jnp.int32),
        grid=(t // tb,),
        in_specs=[hbm, vrow, tok, vcol, _table_spec(tbl_u)],
        out_specs=vrow,
        scratch_shapes=route_scratch + [prod, prod],
        compiler_params=_cparams(1, VMEM_LIMIT),
        name="peer_act",
    )(idx, idx, n2_r, gate_t, tbl_u)


def _peer_out(pk, h_r, tbl_v):
    t = h_r.shape[0]
    tb, hbm, vcol, vrow, tok, route_scratch = _peer_specs(t)
    return pl.pallas_call(
        _peer_out_kernel,
        out_shape=jax.ShapeDtypeStruct(h_r.shape, F32),
        grid=(t // tb,),
        in_specs=[hbm, tok, _table_spec(tbl_v)],
        out_specs=tok,
        scratch_shapes=route_scratch,
        compiler_params=_cparams(1, VMEM_LIMIT),
        name="peer_out",
    )(pk, h_r, tbl_v)


def _pack_table_kernel(t_ref, o_ref):
    half = D_MODEL // 2
    as_bits = lambda x: pltpu.bitcast(x.astype(BF16).astype(F32), jnp.int32)
    lo = lax.shift_right_logical(as_bits(t_ref[:, :half]), 16)
    words = as_bits(t_ref[:, half:]) | lo
    n = t_ref.shape[0]
    for r in range(ROWS_PER_EXPERT):
        o_ref[pl.ds(r, n, stride=ROWS_PER_EXPERT), :] = words[:, r * LANES:(r + 1) * LANES]


def _pack_table(tbl):
    n_exp = tbl.shape[0]
    blk = 512
    assert n_exp % blk == 0
    return pl.pallas_call(
        _pack_table_kernel,
        out_shape=jax.ShapeDtypeStruct((n_exp * ROWS_PER_EXPERT, LANES), jnp.int32),
        grid=(n_exp // blk,),
        in_specs=[pl.BlockSpec((blk, D_MODEL), lambda i: (i, 0))],
        out_specs=pl.BlockSpec((blk * ROWS_PER_EXPERT, LANES), lambda i: (i, 0)),
        compiler_params=_cparams(1),
        name="pack_table",
    )(tbl)


def _prep_params(norm_mix, w_in, b_forget, fox_q_norm, fox_k_norm, w_o_fox, w_o_sb, w_out,
                 norm_ffn, w_peer_q, keys1, keys2, expert_u, expert_v):
    w = w_in.astype(BF16)
    o_f = 3 * WIDTH
    o_b = o_f + N_HEADS
    o_g = o_b + 3 * WIDTH
    pad = LANES - N_HEADS
    head = lax.broadcasted_iota(jnp.int32, (WIDTH, WIDTH), 0) // HEAD_DIM
    head_t = lax.broadcasted_iota(jnp.int32, (WIDTH, WIDTH), 1) // HEAD_DIM
    return {
        "g_mix": norm_mix.reshape(1, D_MODEL),
        "w_a": w[:, :o_f],
        "w_f": jnp.pad(w[:, o_f:o_b], ((0, 0), (0, pad))),
        "w_b": w[:, o_b:o_g],
        "w_g": w[:, o_g:],
        "b_f": jnp.pad(b_forget.reshape(1, N_HEADS), ((0, 0), (0, pad))),
        "qn": jnp.tile(fox_q_norm.reshape(1, HEAD_DIM), (1, N_HEADS)),
        "kn": jnp.tile(fox_k_norm.reshape(1, HEAD_DIM), (1, N_HEADS)),
        "bd": (head == head_t).astype(BF16),
        "w_o_fox": w_o_fox.astype(BF16),
        "w_o_sb": w_o_sb.astype(BF16),
        "w_out": w_out.astype(BF16),
        "g_ffn": norm_ffn.reshape(1, D_MODEL),
        "w_peer_q": w_peer_q.astype(BF16),
        "keys1": keys1.astype(BF16),
        "keys2": keys2.astype(BF16),
        "tbl_u": _pack_table(expert_u),
        "tbl_v": _pack_table(expert_v),
    }


def _forget_bias_layouts(lf_past, lf_new, tq):
    bsz, lq, _ = lf_new.shape
    p_len = lf_past.shape[1]
    total = p_len + lq
    lp = -(-total // LANES) * LANES
    lf_all = jnp.concatenate([jnp.broadcast_to(lf_past, (bsz, p_len, N_HEADS)), lf_new], axis=1)
    rows = jnp.pad(jnp.transpose(lf_all, (0, 2, 1)), ((0, 0), (0, 0), (0, lp - total)))
    c = _logf_cumsum(rows.reshape(bsz * N_HEADS, lp)).reshape(bsz, N_PAIRS, 2, lp)
    c_new = c[..., p_len:total]
    cn =jnp.transpose(c_new.reshape(bsz, N_PAIRS, 2, lq // tq, tq), (0, 1, 3, 2, 4))
    cp = c[..., :p_len]
    return cn, cp


def _group_forward(x, past, p):
    bsz, lq, _ = x.shape
    t = bsz * lq
    x2 = x.reshape(t, D_MODEL)
    fq, fk, fv, lf, sq, sk, sv, ga, gb = _in_proj(x2, p)
    b3 = lambda a: a.reshape(bsz, lq, a.shape[-1])
    pk, pv, plf, psk, psv = past
    tq = min(ATTN_BLOCK, lq)
    cn, cp = _forget_bias_layouts(plf, b3(lf), tq)
    o_fox = _fox_attn(b3(fq), b3(fk), b3(fv), pk, pv, cn, cp)
    o_sb = _sb_attn(b3(sq), b3(sk), b3(sv), psk, psv)
    h, n2, idx, gate_t = _mix_route(x2, o_fox.reshape(t, WIDTH), o_sb.reshape(t, WIDTH), ga, gb, p)
    tile = lambda a: a.reshape(t, SUBLANES, LANES)
    routes = _peer_act(idx, tile(n2), gate_t, p["tbl_u"])
    out = _peer_out(routes, tile(h), p["tbl_v"])
    return out.reshape(bsz, lq, D_MODEL), (b3(fk), b3(fv), b3(lf), b3(sk), b3(sv))


def kernel(x_prompt, x_sample, cache_fox_k, cache_fox_v, cache_fox_logf, cache_sb_k, cache_sb_v,
           meta_tokens, norm_mix, w_in, b_forget, fox_q_norm, fox_k_norm, w_o_fox, w_o_sb, w_out,
           norm_ffn, w_peer_q, peer_sub_keys_1, peer_sub_keys_2, expert_u, expert_v):
    assert norm_mix.shape[0] == 1, "single-layer stack"
    p = _prep_params(norm_mix[0], w_in[0], b_forget[0], fox_q_norm[0], fox_k_norm[0], w_o_fox[0],
                     w_o_sb[0], w_out[0], norm_ffn[0], w_peer_q[0], peer_sub_keys_1[0],
                     peer_sub_keys_2[0], expert_u[0], expert_v[0])
    bsz = x_prompt.shape[0]
    n_meta = meta_tokens.shape[0]

    _, mk, mv, mlf, _, msk, msv, _, _ = _in_proj(meta_tokens.astype(x_prompt.dtype), p)
    meta_past = (mk[None], mv[None], mlf[None], msk[None], msv[None])
    y_prompt, rows_p = _group_forward(x_prompt, meta_past, p)

    def with_meta(meta_rows, new_rows, tail_shape):
        full = jnp.concatenate([jnp.broadcast_to(meta_rows[None], (bsz,) + meta_rows.shape), new_rows], axis=1)
        return full.reshape((1, bsz, n_meta + new_rows.shape[1]) + tail_shape)

    hd = (N_HEADS, HEAD_DIM)
    out_p = (with_meta(mk, rows_p[0], hd), with_meta(mv, rows_p[1], hd), with_meta(mlf, rows_p[2], (N_HEADS,)),
             with_meta(msk, rows_p[3], hd), with_meta(msv, rows_p[4], hd))

    dbsz, plen = cache_fox_k.shape[1], cache_fox_k.shape[2]
    flat = lambda c: c[0].reshape(dbsz, plen, -1)
    sample_past = (flat(cache_fox_k), flat(cache_fox_v), flat(cache_fox_logf), flat(cache_sb_k), flat(cache_sb_v))
    y_sample, rows_s = _group_forward(x_sample, sample_past, p)
    lq = x_sample.shape[1]
    out_s = (rows_s[0].reshape(1, dbsz, lq, *hd), rows_s[1].reshape(1, dbsz, lq, *hd),
             rows_s[2].reshape(1, dbsz, lq, N_HEADS), rows_s[3].reshape(1, dbsz, lq, *hd),
             rows_s[4].reshape(1, dbsz, lq, *hd))
    return (y_prompt, y_sample) + out_p + out_s
```

```python
import functools

import jax
import jax.numpy as jnp
from jax import lax
from jax.experimental import pallas as pl
from jax.experimental.pallas import tpu as pltpu

F32 = jnp.float32
BF16 = jnp.bfloat16

D_MODEL = 1024
HEAD_DIM = 64
N_HEADS = 8
WIDTH = N_HEADS * HEAD_DIM
N_PAIRS = N_HEADS // 2
NORM_EPS = 1e-6
MASK_VALUE = -1e30
ATTN_SCALE = HEAD_DIM ** -0.5

PEER_HEADS = 8
PEER_N_KEYS = 128
PEER_HALF = 128
PEER_TOPK = 16
PEER_PICKS = PEER_HEADS * PEER_TOPK

LANES = 128
SUBLANES = 8
ROWS_PER_EXPERT = D_MODEL // 2 // LANES

TOKEN_BLOCK = 256
ATTN_BLOCK = 256
PAST_CHUNK_FOX = 512
PAST_CHUNK_SB = 128
PEER_TOKEN_BLOCK = 128
ACT_RING = (4, 8)
OUT_RING = (4, 4)
VMEM_LIMIT = 56 * 1024 * 1024


def _cparams(n_grid, vmem=None):
    return pltpu.CompilerParams(
        dimension_semantics=("arbitrary",) * n_grid,
        vmem_limit_bytes=vmem,
    )


def _dot(a, b):
    return jnp.dot(a, b, preferred_element_type=F32)


def _dot_nt(a, b):
    return lax.dot_general(a, b, (((1,), (1,)), ((), ())), preferred_element_type=F32)


def _split_dot(a, b_bf16, terms):
    out = None
    rem = a
    for i in range(terms):
        part = rem.astype(BF16)
        d = _dot(part, b_bf16)
        out = d if out is None else out + d
        if i + 1 < terms:
            rem = rem - part.astype(F32)
    return out


def _log_sigmoid(x):
    return jnp.minimum(x, 0.0) - jnp.log(1.0 + jnp.exp(-jnp.abs(x)))


def _sigmoid(x):
    return 1.0 / (1.0 + jnp.exp(-x))


def _in_proj_kernel(x_ref, g_ref, wa_ref, wf_ref, wb_ref, wg_ref, bf_ref, qn_ref, kn_ref, bd_ref,
                    fq_ref, fk_ref, fv_ref, lf_ref, sq_ref, sk_ref, sv_ref, ga_ref, gb_ref):
    x = x_ref[...]
    ms = jnp.mean(x * x, axis=-1, keepdims=True)
    n = (x * lax.rsqrt(ms + NORM_EPS) * g_ref[...]).astype(BF16)

    a = _dot(n, wa_ref[...])
    bd = bd_ref[...]

    def head_norm(t, gain):
        msq = _split_dot(t * t, bd, 2) * (1.0 / HEAD_DIM)
        return t * lax.rsqrt(msq + NORM_EPS) * gain

    fq_ref[...] = head_norm(a[:, :WIDTH], qn_ref[...])
    fk_ref[...] = head_norm(a[:, WIDTH:2 * WIDTH], kn_ref[...])
    fv_ref[...] = a[:, 2 * WIDTH:]

    f = _dot(n, wf_ref[...]) + bf_ref[...]
    lf_ref[...] = _log_sigmoid(f)[:, :N_HEADS]

    b = _dot(n, wb_ref[...])
    sq_ref[...] = b[:, :WIDTH]
    sk_ref[...] = b[:, WIDTH:2 * WIDTH]
    sv_ref[...] = b[:, 2 * WIDTH:]

    g = _dot(n, wg_ref[...])
    ga_ref[...] = _sigmoid(g[:, :D_MODEL])
    gb_ref[...] = _sigmoid(g[:, D_MODEL:])


def _in_proj(x, p):
    t = x.shape[0]
    tl = min(TOKEN_BLOCK, t)
    assert t % tl == 0
    row = lambda w: pl.BlockSpec((tl, w), lambda i: (i, 0))
    full = lambda a: pl.BlockSpec(a.shape, lambda i: (0,) * a.ndim)
    consts = (p["g_mix"], p["w_a"], p["w_f"], p["w_b"], p["w_g"], p["b_f"], p["qn"], p["kn"], p["bd"])
    widths = (WIDTH, WIDTH, WIDTH, N_HEADS, WIDTH, WIDTH, WIDTH, D_MODEL, D_MODEL)
    return pl.pallas_call(
        _in_proj_kernel,
        out_shape=tuple(jax.ShapeDtypeStruct((t, w), F32) for w in widths),
        grid=(t // tl,),
        in_specs=[row(D_MODEL)] + [full(c) for c in consts],
        out_specs=tuple(row(w) for w in widths),
        compiler_params=_cparams(1, VMEM_LIMIT),
        name="in_proj",
    )(x, *consts)


def _cumsum_kernel(x_ref, o_ref):
    n_chunks = x_ref.shape[1] // LANES
    r = lax.broadcasted_iota(jnp.int32, (LANES, LANES), 0)
    c = lax.broadcasted_iota(jnp.int32, (LANES, LANES), 1)
    upper = jnp.where(r <= c, 1.0, 0.0).astype(BF16)
    carry = jnp.zeros((x_ref.shape[0], 1), F32)
    for k in range(n_chunks):
        y = _split_dot(x_ref[:, k * LANES:(k + 1) * LANES], upper, 3) + carry
        o_ref[:, k * LANES:(k + 1) * LANES] = y
        carry = y[:, LANES - 1:LANES]


def _logf_cumsum(lf_rows):
    return pl.pallas_call(
        _cumsum_kernel,
        out_shape=jax.ShapeDtypeStruct(lf_rows.shape, F32),
        name="logf_cumsum",
    )(lf_rows)


def _head_masks():
    lane = lax.broadcasted_iota(jnp.int32, (1, LANES), 1)
    return lane < HEAD_DIM, lane >= HEAD_DIM


def _stack_heads(x):
    m0, m1 = _head_masks()
    return jnp.concatenate([jnp.where(m0, x, 0.0), jnp.where(m1, x, 0.0)], axis=0)


def _unstack_heads(y, tq):
    m0, _ = _head_masks()
    return jnp.where(m0, y[:tq], y[tq:])


def _stacked_pos(tq, tk):
    r = lax.broadcasted_iota(jnp.int32, (2 * tq, tk), 0)
    c = lax.broadcasted_iota(jnp.int32, (2 * tq, tk), 1)
    return jnp.where(r >= tq, r - tq, r), c


def _static_chunks(n, size):
    out = [(s, size) for s in range(0, n - size + 1, size)]
    done = len(out) * size
    if done < n:
        out.append((done, n - done))
    return out


def _tri_strict(n):
    r = lax.broadcasted_iota(jnp.int32, (n, n), 0)
    c = lax.broadcasted_iota(jnp.int32, (n, n), 1)
    return jnp.where(r > c, 1.0, 0.0).astype(BF16)


def _softmax_step(carry, qs, is_head1, segments):
    scores = []
    for kb, _, ck, valid in segments:
        bias = jnp.where(is_head1, ck[1:2, :], ck[0:1, :])
        s = _dot_nt(qs, kb.astype(BF16)) - bias
        if valid is not None:
            s = jnp.where(valid, s, MASK_VALUE)
        scores.append(s)
    m_new = functools.reduce(jnp.maximum, [jnp.max(s, axis=-1, keepdims=True) for s in scores])
    if carry is not None:
        m_old, l_old, acc_old = carry
        m_new = jnp.maximum(m_old, m_new)
    l = None
    acc = None
    for s, (_, vb, _, _) in zip(scores, segments):
        pr = jnp.exp(s - m_new)
        ls = jnp.sum(pr, axis=-1, keepdims=True)
        pv = _dot(pr.astype(BF16), vb.astype(BF16))
        l = ls if l is None else l + ls
        acc = pv if acc is None else acc + pv
    if carry is not None:
        alpha = jnp.exp(m_old - m_new)
        l = alpha * l_old + l
        acc = alpha * acc_old + acc
    return m_new, l, acc


def _fox_kernel(q_ref, kn_ref, vn_ref, kp_ref, vp_ref, cn_ref, cp_ref, o_ref, *, tq, p_len):
    lq = q_ref.shape[1]
    nq = lq // tq
    row, col = _stacked_pos(tq, tq)
    causal = col <= row
    is_head1 = lax.broadcasted_iota(jnp.int32, (2 * tq, 1), 0) >= tq
    past_size = p_len if nq > 1 else PAST_CHUNK_FOX
    past = [(kp_ref[0, s:s + n, :], vp_ref[0, s:s + n, :], cp_ref[0, 0, :, s:s + n], None)
            for s, n in _static_chunks(p_len, past_size)]

    def q_block(i):
        q0 = i * tq if nq == 1 else pl.multiple_of(i * tq, tq)
        qs = _stack_heads(q_ref[0, pl.ds(q0, tq), :] * ATTN_SCALE).astype(BF16)
        diag = (kn_ref[0, pl.ds(q0, tq), :], vn_ref[0, pl.ds(q0, tq), :], cn_ref[0, 0, i], causal)
        carry = _softmax_step(None, qs, is_head1, past + [diag])

        if nq > 1:
            def seg_of(j):
                k0 = pl.multiple_of(j * tq, tq)
                return (kn_ref[0, pl.ds(k0, tq), :], vn_ref[0, pl.ds(k0, tq), :], cn_ref[0, 0, j], None)

            def pair_body(jj, carry):
                return _softmax_step(carry, qs, is_head1, [seg_of(2 * jj), seg_of(2 * jj + 1)])

            carry = lax.fori_loop(0, i // 2, pair_body, carry)
            carry = lax.cond(i % 2 == 1, lambda c: _softmax_step(c, qs, is_head1, [seg_of(i - 1)]),
                             lambda c: c, carry)
        _, l, acc = carry
        o_ref[0, pl.ds(q0, tq), :] = _unstack_heads(acc / l, tq)

    if nq == 1:
        q_block(0)
    else:
        lax.fori_loop(0, nq, lambda i, _: (q_block(i), 0)[1], 0)


def _sb_kernel(q_ref, kn_ref, vn_ref, kp_ref, vp_ref, o_ref, *, tq, p_len):
    lq = q_ref.shape[1]
    nq = lq // tq
    row, col = _stacked_pos(tq, tq)
    strictly_before = col < row
    tri_q = _tri_strict(tq)

    def local(qs, kb, tri_m, valid=None):
        z = _dot_nt(qs, kb.astype(BF16))
        m = _log_sigmoid(-z)
        if valid is not None:
            m = jnp.where(valid, m, 0.0)
        tail_in = _split_dot(m, tri_m, 2)
        return z, m, tail_in, tail_in[:, 0:1] + m[:, 0:1]

    def weights(z, m, tail_in, run, valid=None):
        a = jnp.exp(z + m + tail_in) if run is None else jnp.exp(z + m + tail_in + run)
        if valid is not None:
            a = jnp.where(valid, a, 0.0)
        return a.astype(BF16)

    def q_block(i):
        q0 = i * tq if nq == 1 else pl.multiple_of(i * tq, tq)
        qs = _stack_heads(q_ref[0, pl.ds(q0, tq), :] * ATTN_SCALE).astype(BF16)
        z, m, tail_in, tot = local(qs, kn_ref[0, pl.ds(q0, tq), :], tri_q, strictly_before)
        acc = _dot(weights(z, m, tail_in, None, strictly_before), vn_ref[0, pl.ds(q0, tq), :].astype(BF16))
        run = tot

        if nq > 1:
            assert p_len <= ATTN_BLOCK
            zp, mp, tp, _ = local(qs, kp_ref[0], _tri_strict(p_len))
            past_acc = _dot(weights(zp, mp, tp, None), vp_ref[0].astype(BF16))

            def blk(j):
                k0 = pl.multiple_of(j * tq, tq)
                return kn_ref[0, pl.ds(k0, tq), :], vn_ref[0, pl.ds(k0, tq), :].astype(BF16)

            def pair_body(jj, carry):
                run, acc = carry
                k1, v1 = blk(i - 1 - 2 * jj)
                k2, v2 = blk(i - 2 - 2 * jj)
                z1, m1, t1, tot1 = local(qs, k1, tri_q)
                z2, m2, t2, tot2 = local(qs, k2, tri_q)
                acc = acc + _dot(weights(z1, m1, t1, run), v1) + _dot(weights(z2, m2, t2, run + tot1), v2)
                return run + (tot1 + tot2), acc

            def last_body(carry):
                run, acc = carry
                k1, v1 = blk(0)
                z1, m1, t1, tot1 = local(qs, k1, tri_q)
                return run + tot1, acc + _dot(weights(z1, m1, t1, run), v1)

            run, acc = lax.fori_loop(0, i // 2, pair_body, (run, acc))
            run, acc = lax.cond(i % 2 == 1, last_body, lambda c: c, (run, acc))
            acc = acc + jnp.exp(run) * past_acc
        else:
            chunks = _static_chunks(p_len, PAST_CHUNK_SB)
            tris = {n: _tri_strict(n) for n in {n for _, n in chunks}}
            for s, n in reversed(chunks):
                z, m, tail_in, tot = local(qs, kp_ref[0, s:s + n, :], tris[n])
                acc = acc + _dot(weights(z, m, tail_in, run), vp_ref[0, s:s + n, :].astype(BF16))
                run = run + tot
        o_ref[0, pl.ds(q0, tq), :] = _unstack_heads(acc, tq)

    if nq == 1:
        q_block(0)
    else:
        lax.fori_loop(0, nq, lambda i, _: (q_block(i), 0)[1], 0)


def _attn_specs(bsz, lq, p_len, past_shared):
    pair_new = pl.BlockSpec((1, lq, LANES), lambda b, p: (b, 0, p))
    if past_shared:
        pair_past = pl.BlockSpec((1, p_len, LANES), lambda b, p: (0, 0, p))
    else:
        pair_past = pl.BlockSpec((1, p_len, LANES), lambda b, p: (b, 0, p))
    return pair_new, pair_past


def _fox_attn(q, k_new, v_new, k_past, v_past, cn, cp):
    bsz, lq, _ = q.shape
    p_len = k_past.shape[1]
    tq = min(ATTN_BLOCK, lq)
    pair_new, pair_past = _attn_specs(bsz, lq, p_len, k_past.shape[0] == 1)
    c_spec = lambda a: pl.BlockSpec((1, 1) + a.shape[2:], lambda b, p: (b, p) + (0,) * (a.ndim - 2))
    return pl.pallas_call(
        functools.partial(_fox_kernel, tq=tq, p_len=p_len),
        out_shape=jax.ShapeDtypeStruct(q.shape, F32),
        grid=(bsz, N_PAIRS),
        in_specs=[pair_new, pair_new, pair_new, pair_past, pair_past, c_spec(cn), c_spec(cp)],
        out_specs=pair_new,
        compiler_params=_cparams(2, VMEM_LIMIT),
        name="fox_attn",
    )(q, k_new, v_new, k_past, v_past, cn, cp)


def _sb_attn(q, k_new, v_new, k_past, v_past):
    bsz, lq, _ = q.shape
    p_len = k_past.shape[1]
    tq = min(ATTN_BLOCK, lq)
    pair_new, pair_past = _attn_specs(bsz, lq, p_len, k_past.shape[0] == 1)
    return pl.pallas_call(
        functools.partial(_sb_kernel, tq=tq, p_len=p_len),
        out_shape=jax.ShapeDtypeStruct(q.shape, F32),
        grid=(bsz, N_PAIRS),
        in_specs=[pair_new, pair_new, pair_new, pair_past, pair_past],
        out_specs=pair_new,
        compiler_params=_cparams(2, VMEM_LIMIT),
        name="sb_attn",
    )(q, k_new, v_new, k_past, v_past)


def _topk_rows(vals, payload, k):
    n = vals.shape[0]
    rio = lax.broadcasted_iota(jnp.int32, vals.shape, 0)
    out_v, out_p = [], []
    for _ in range(k):
        mx = jnp.max(vals, axis=0, keepdims=True)
        pos = jnp.min(jnp.where(vals == mx, rio, n), axis=0, keepdims=True)
        sel = rio == pos
        out_v.append(mx)
        if payload is None:
            out_p.append(pos)
        else:
            out_p.append(jnp.sum(jnp.where(sel, payload, 0), axis=0, keepdims=True))
        vals = jnp.where(sel, -jnp.inf, vals)
    return jnp.concatenate(out_v, axis=0), jnp.concatenate(out_p, axis=0)


def _pair_rows(first, second):
    half = PEER_TOPK // 2
    rows = [first[0:1, :] + second]
    rows += [first[a:a + 1, :] + second[:half] for a in range(1, half)]
    rows.append(first[half:, :] + second[0:1, :])
    return jnp.concatenate(rows, axis=0)


def _mix_route_kernel(x_ref, of_ref, os_ref, ga_ref, gb_ref, wof_ref, wos_ref, wout_ref, gf_ref,
                      wq_ref, k1_ref, k2_ref, h_ref, n2_ref, idx_ref, gate_ref):
    yf = _dot(of_ref[...].astype(BF16), wof_ref[...])
    ys = _dot(os_ref[...].astype(BF16), wos_ref[...])
    merged = ga_ref[...] * yf + gb_ref[...] * ys
    h = x_ref[...] + _dot(merged.astype(BF16), wout_ref[...])
    h_ref[...] = h
    ms = jnp.mean(h * h, axis=-1, keepdims=True)
    n2 = h * lax.rsqrt(ms + NORM_EPS) * gf_ref[...]
    n2_ref[...] = n2
    qp = _dot(n2.astype(BF16), wq_ref[...]).astype(BF16)
    k1 = k1_ref[...]
    k2 = k2_ref[...]
    tl = x_ref.shape[0]
    for c0 in range(0, tl, LANES):
        rows = []
        for hd in range(PEER_HEADS):
            base = hd * 2 * PEER_HALF
            q1 = qp[c0:c0 + LANES, base:base + PEER_HALF]
            q2 = qp[c0:c0 + LANES, base + PEER_HALF:base + 2 * PEER_HALF]
            s1 = _dot_nt(k1, q1)
            s2 = _dot_nt(k2, q2)
            v1, i1 = _topk_rows(s1, None, PEER_TOPK)
            v2, i2 = _topk_rows(s2, None, PEER_TOPK)
            sc, e = _topk_rows(_pair_rows(v1, v2), _pair_rows(i1 * PEER_N_KEYS, i2), PEER_TOPK)
            ex = jnp.exp(sc - jnp.max(sc, axis=0, keepdims=True))
            gate = ex / jnp.sum(ex, axis=0, keepdims=True)
            r0 = hd * PEER_TOPK
            rows.append(e * ROWS_PER_EXPERT)
            gate_ref[r0:r0 + PEER_TOPK, c0:c0 + LANES] = gate
        idx_ref[c0:c0 + LANES, :] = jnp.concatenate(rows, axis=0).T


def _mix_route(x, o_fox, o_sb, ga, gb, p):
    t = x.shape[0]
    tl = min(TOKEN_BLOCK, t)
    assert t % tl == 0 and tl % LANES == 0
    row = lambda w: pl.BlockSpec((tl, w), lambda i: (i, 0))
    full = lambda a: pl.BlockSpec(a.shape, lambda i: (0,) * a.ndim)
    colblk = pl.BlockSpec((PEER_PICKS, tl), lambda i: (0, i))
    consts_a = (p["w_o_fox"], p["w_o_sb"], p["w_out"], p["g_ffn"], p["w_peer_q"], p["keys1"], p["keys2"])
    return pl.pallas_call(
        _mix_route_kernel,
        out_shape=(jax.ShapeDtypeStruct((t, D_MODEL), F32), jax.ShapeDtypeStruct((t, D_MODEL), F32),
                   jax.ShapeDtypeStruct((t, PEER_PICKS), jnp.int32), jax.ShapeDtypeStruct((PEER_PICKS, t), F32)),
        grid=(t // tl,),
        in_specs=[row(D_MODEL), row(WIDTH), row(WIDTH), row(D_MODEL), row(D_MODEL)] + [full(c) for c in consts_a],
        out_specs=(row(D_MODEL), row(D_MODEL), row(PEER_PICKS), colblk),
        compiler_params=_cparams(1, VMEM_LIMIT),
        name="mix_route",
    )(x, o_fox, o_sb, ga, gb, *consts_a)


def _unpack_pair(words):
    lo = pltpu.bitcast(words << 16, F32)
    hi = pltpu.bitcast(words & jnp.int32(-65536), F32)
    return lo, hi


def _route_copy(src_hbm, slots, sems, chunk, slot):
    rows = slots.shape[1]
    row0 = pl.multiple_of(chunk * rows, rows)
    return pltpu.make_async_copy(src_hbm.at[pl.ds(row0, rows)], slots.at[slot], sems.at[slot])


def _for_each_token(route_hbm, slots, sems, token_fn, carry):
    n_slots, rows, _ = slots.shape
    step = pl.program_id(0)
    chunks_per_step = PEER_TOKEN_BLOCK // rows
    total_chunks = pl.num_programs(0) * chunks_per_step
    ahead = n_slots - 1

    @pl.when(step == 0)
    def _():
        for c in range(ahead):
            _route_copy(route_hbm, slots, sems, c, c).start()

    def slot_round(g, carry):
        for slot in range(n_slots):
            local = g * n_slots + slot
            chunk = step * chunks_per_step + local

            @pl.when(chunk + ahead < total_chunks)
            def _():
                _route_copy(route_hbm, slots, sems, chunk + ahead, (slot + ahead) % n_slots).start()

            _route_copy(route_hbm, slots, sems, chunk, slot).wait()
            for u in range(rows):
                carry = token_fn(local * rows + u, slot, u, carry)
        return carry

    return lax.fori_loop(0, chunks_per_step // n_slots, slot_round, carry)


def _gather_pair(tbl_ref, row_a, row_b):
    ra = pl.multiple_of(row_a, ROWS_PER_EXPERT)
    rb = pl.multiple_of(row_b, ROWS_PER_EXPERT)
    words = jnp.concatenate([tbl_ref[pl.ds(ra, ROWS_PER_EXPERT), :], tbl_ref[pl.ds(rb, ROWS_PER_EXPERT), :]], axis=0)
    return _unpack_pair(words)


def _peer_act_kernel(idx_hbm, idx_ref, x_ref, gate_ref, tbl_ref, pk_ref, slots, sems, prod_a, prod_b):
    tb = x_ref.shape[0]
    lane = lax.broadcasted_iota(jnp.int32, (PEER_PICKS, tb), 1)
    prods = (prod_a, prod_b)
    prod_b[...] = jnp.zeros_like(prod_b)

    def reduce(t, prod_ref, dots):
        part = prod_ref[pl.ds(0, PEER_PICKS, stride=ROWS_PER_EXPERT), :]
        for r in range(1, ROWS_PER_EXPERT):
            part = part + prod_ref[pl.ds(r, PEER_PICKS, stride=ROWS_PER_EXPERT), :]
        d = jnp.sum(part, axis=-1, keepdims=True)
        return jnp.where(lane == t, d, dots)

    def token(t, slot, u, dots):
        xt = x_ref[t]
        x_lo = jnp.concatenate([xt[:ROWS_PER_EXPERT]] * 2, axis=0)
        x_hi = jnp.concatenate([xt[ROWS_PER_EXPERT:]] * 2, axis=0)
        prod_ref = prods[u % 2]
        for j in range(0, PEER_PICKS, 2):
            lo, hi = _gather_pair(tbl_ref, slots[slot, u, j], slots[slot, u, j + 1])
            prod_ref[j * ROWS_PER_EXPERT:(j + 2) * ROWS_PER_EXPERT, :] = lo * x_lo + hi * x_hi
        return reduce(t - 1, prods[1 - u % 2], dots)

    dots = _for_each_token(idx_hbm, slots, sems, token, jnp.zeros((PEER_PICKS, tb), F32))
    dots = reduce(tb - 1, prods[(tb - 1) % 2], dots)
    act = 0.5 * dots * (1.0 + lax.erf(dots * (2.0 ** -0.5)))
    w = (gate_ref[...] * act).T.astype(BF16).astype(F32)
    pk_ref[...] = pltpu.bitcast(w, jnp.int32) | idx_ref[...]


def _peer_out_kernel(pk_hbm, h_ref, tbl_ref, o_ref, slots, sems):
    n_acc = 2
    upper = lax.broadcasted_iota(jnp.int32, (SUBLANES, LANES), 0) >= ROWS_PER_EXPERT

    def token(t, slot, u, carry):
        acc_lo = [jnp.zeros((SUBLANES, LANES), F32) for _ in range(n_acc)]
        acc_hi = [jnp.zeros((SUBLANES, LANES), F32) for _ in range(n_acc)]
        for j in range(0, PEER_PICKS, 2):
            wa = slots[slot, u, j]
            wb = slots[slot, u, j + 1]
            lo, hi = _gather_pair(tbl_ref, wa & 0xFFFF, wb & 0xFFFF)
            wv = pltpu.bitcast(jnp.where(upper, wb, wa) & jnp.int32(-65536), F32)
            k = (j // 2) % n_acc
            acc_lo[k] = acc_lo[k] + wv * lo
            acc_hi[k] = acc_hi[k] + wv * hi
        lo = acc_lo[0] + acc_lo[1]
        hi = acc_hi[0] + acc_hi[1]
        peer = jnp.concatenate([lo[:ROWS_PER_EXPERT] + lo[ROWS_PER_EXPERT:],
                                hi[:ROWS_PER_EXPERT] + hi[ROWS_PER_EXPERT:]], axis=0)
        o_ref[t] = h_ref[t] + peer
        return carry

    _for_each_token(pk_hbm, slots, sems, token, 0)


def _peer_specs(t, ring):
    tb = PEER_TOKEN_BLOCK
    n_slots, rows = ring
    assert t % tb == 0 and tb % (rows * n_slots) == 0 and rows % 2 == 0
    hbm = pl.BlockSpec(memory_space=pl.ANY)
    vcol = pl.BlockSpec((PEER_PICKS, tb), lambda i: (0, i))
    vrow = pl.BlockSpec((tb, PEER_PICKS), lambda i: (i, 0))
    tok = pl.BlockSpec((tb, SUBLANES, LANES), lambda i: (i, 0, 0))
    route_scratch = [pltpu.SMEM((n_slots, rows, PEER_PICKS), jnp.int32), pltpu.SemaphoreType.DMA((n_slots,))]
    return tb, hbm, vcol, vrow, tok, route_scratch


def _table_spec(tbl):
    return pl.BlockSpec(tbl.shape, lambda i: (0, 0), pipeline_mode=pl.Buffered(1))


def _peer_act(idx, n2_r, gate_t, tbl_u):
    t = n2_r.shape[0]
    tb, hbm, vcol, vrow, tok, route_scratch = _peer_specs(t, ACT_RING)
    prod = pltpu.VMEM((PEER_PICKS * ROWS_PER_EXPERT, LANES), F32)
    return pl.pallas_call(
        _peer_act_kernel,
        out_shape=jax.ShapeDtypeStruct((t, PEER_PICKS), jnp.int32),
        grid=(t // tb,),
        in_specs=[hbm, vrow, tok, vcol, _table_spec(tbl_u)],
        out_specs=vrow,
        scratch_shapes=route_scratch + [prod, prod],
        compiler_params=_cparams(1, VMEM_LIMIT),
        name="peer_act",
    )(idx, idx, n2_r, gate_t, tbl_u)


def _peer_out(pk, h_r, tbl_v):
    t = h_r.shape[0]
    tb, hbm, vcol, vrow, tok, route_scratch = _peer_specs(t, OUT_RING)
    return pl.pallas_call(
        _peer_out_kernel,
        out_shape=jax.ShapeDtypeStruct(h_r.shape, F32),
        grid=(t // tb,),
        in_specs=[hbm, tok, _table_spec(tbl_v)],
        out_specs=tok,
        scratch_shapes=route_scratch,
        compiler_params=_cparams(1, VMEM_LIMIT),
        name="peer_out",
    )(pk, h_r, tbl_v)


def _pack_table_kernel(t_ref, o_ref):
    half = D_MODEL // 2
    as_bits = lambda x: pltpu.bitcast(x.astype(BF16).astype(F32), jnp.int32)
    lo = lax.shift_right_logical(as_bits(t_ref[:, :half]), 16)
    words = as_bits(t_ref[:, half:]) | lo
    n = t_ref.shape[0]
    for r in range(ROWS_PER_EXPERT):
        o_ref[pl.ds(r, n, stride=ROWS_PER_EXPERT), :] = words[:, r * LANES:(r + 1) * LANES]


def _pack_table(tbl):
    n_exp = tbl.shape[0]
    blk = 512
    assert n_exp % blk == 0
    return pl.pallas_call(
        _pack_table_kernel,
        out_shape=jax.ShapeDtypeStruct((n_exp * ROWS_PER_EXPERT, LANES), jnp.int32),
        grid=(n_exp // blk,),
        in_specs=[pl.BlockSpec((blk, D_MODEL), lambda i: (i, 0))],
        out_specs=pl.BlockSpec((blk * ROWS_PER_EXPERT, LANES), lambda i: (i, 0)),
        compiler_params=_cparams(1),
        name="pack_table",
    )(tbl)


def _prep_params(norm_mix, w_in, b_forget, fox_q_norm, fox_k_norm, w_o_fox, w_o_sb, w_out,
                 norm_ffn, w_peer_q, keys1, keys2, expert_u, expert_v):
    w = w_in.astype(BF16)
    o_f = 3 * WIDTH
    o_b = o_f + N_HEADS
    o_g = o_b + 3 * WIDTH
    pad = LANES - N_HEADS
    head = lax.broadcasted_iota(jnp.int32, (WIDTH, WIDTH), 0) // HEAD_DIM
    head_t = lax.broadcasted_iota(jnp.int32, (WIDTH, WIDTH), 1) // HEAD_DIM
    return {
        "g_mix": norm_mix.reshape(1, D_MODEL),
        "w_a": w[:, :o_f],
        "w_f": jnp.pad(w[:, o_f:o_b], ((0, 0), (0, pad))),
        "w_b": w[:, o_b:o_g],
        "w_g": w[:, o_g:],
        "b_f": jnp.pad(b_forget.reshape(1, N_HEADS), ((0, 0), (0, pad))),
        "qn": jnp.tile(fox_q_norm.reshape(1, HEAD_DIM), (1, N_HEADS)),
        "kn": jnp.tile(fox_k_norm.reshape(1, HEAD_DIM), (1, N_HEADS)),
        "bd": (head == head_t).astype(BF16),
        "w_o_fox": w_o_fox.astype(BF16),
        "w_o_sb": w_o_sb.astype(BF16),
        "w_out": w_out.astype(BF16),
        "g_ffn": norm_ffn.reshape(1, D_MODEL),
        "w_peer_q": w_peer_q.astype(BF16),
        "keys1": keys1.astype(BF16),
        "keys2": keys2.astype(BF16),
        "tbl_u": _pack_table(expert_u),
        "tbl_v": _pack_table(expert_v),
    }


def _forget_bias_layouts(lf_past, lf_new, tq):
    bsz, lq, _ = lf_new.shape
    p_len = lf_past.shape[1]
    total = p_len + lq
    lp = -(-total // LANES) * LANES
    lf_all = jnp.concatenate([jnp.broadcast_to(lf_past, (bsz, p_len, N_HEADS)), lf_new], axis=1)
    rows = jnp.pad(jnp.transpose(lf_all, (0, 2, 1)), ((0, 0), (0, 0), (0, lp - total)))
    c = _logf_cumsum(rows.reshape(bsz * N_HEADS, lp)).reshape(bsz, N_PAIRS, 2, lp)
    c_new = c[..., p_len:total]
    cn =jnp.transpose(c_new.reshape(bsz, N_PAIRS, 2, lq // tq, tq), (0, 1, 3, 2, 4))
    cp = c[..., :p_len]
    return cn, cp


def _group_forward(x, past, p):
    bsz, lq, _ = x.shape
    t = bsz * lq
    x2 = x.reshape(t, D_MODEL)
    fq, fk, fv, lf, sq, sk, sv, ga, gb = _in_proj(x2, p)
    b3 = lambda a: a.reshape(bsz, lq, a.shape[-1])
    pk, pv, plf, psk, psv = past
    tq = min(ATTN_BLOCK, lq)
    cn, cp = _forget_bias_layouts(plf, b3(lf), tq)
    o_fox = _fox_attn(b3(fq), b3(fk), b3(fv), pk, pv, cn, cp)
    o_sb = _sb_attn(b3(sq), b3(sk), b3(sv), psk, psv)
    h, n2, idx, gate_t = _mix_route(x2, o_fox.reshape(t, WIDTH), o_sb.reshape(t, WIDTH), ga, gb, p)
    tile = lambda a: a.reshape(t, SUBLANES, LANES)
    routes = _peer_act(idx, tile(n2), gate_t, p["tbl_u"])
    out = _peer_out(routes, tile(h), p["tbl_v"])
    return out.reshape(bsz, lq, D_MODEL), (b3(fk), b3(fv), b3(lf), b3(sk), b3(sv))


def kernel(x_prompt, x_sample, cache_fox_k, cache_fox_v, cache_fox_logf, cache_sb_k, cache_sb_v,
           meta_tokens, norm_mix, w_in, b_forget, fox_q_norm, fox_k_norm, w_o_fox, w_o_sb, w_out,
           norm_ffn, w_peer_q, peer_sub_keys_1, peer_sub_keys_2, expert_u, expert_v):
    assert norm_mix.shape[0] == 1, "single-layer stack"
    p = _prep_params(norm_mix[0], w_in[0], b_forget[0], fox_q_norm[0], fox_k_norm[0], w_o_fox[0],
                     w_o_sb[0], w_out[0], norm_ffn[0], w_peer_q[0], peer_sub_keys_1[0],
                     peer_sub_keys_2[0], expert_u[0], expert_v[0])
    bsz = x_prompt.shape[0]
    n_meta = meta_tokens.shape[0]

    _, mk, mv, mlf, _, msk, msv, _, _ = _in_proj(meta_tokens.astype(x_prompt.dtype), p)
    meta_past = (mk[None], mv[None], mlf[None], msk[None], msv[None])
    y_prompt, rows_p = _group_forward(x_prompt, meta_past, p)

    def with_meta(meta_rows, new_rows, tail_shape):
        full = jnp.concatenate([jnp.broadcast_to(meta_rows[None], (bsz,) + meta_rows.shape), new_rows], axis=1)
        return full.reshape((1, bsz, n_meta + new_rows.shape[1]) + tail_shape)

    hd = (N_HEADS, HEAD_DIM)
    out_p = (with_meta(mk, rows_p[0], hd), with_meta(mv, rows_p[1], hd), with_meta(mlf, rows_p[2], (N_HEADS,)),
             with_meta(msk, rows_p[3], hd), with_meta(msv, rows_p[4], hd))

    dbsz, plen = cache_fox_k.shape[1], cache_fox_k.shape[2]
    flat = lambda c: c[0].reshape(dbsz, plen, -1)
    sample_past = (flat(cache_fox_k), flat(cache_fox_v), flat(cache_fox_logf), flat(cache_sb_k), flat(cache_sb_v))
    y_sample, rows_s = _group_forward(x_sample, sample_past, p)
    lq = x_sample.shape[1]
    out_s = (rows_s[0].reshape(1, dbsz, lq, *hd), rows_s[1].reshape(1, dbsz, lq, *hd),
             rows_s[2].reshape(1, dbsz, lq, N_HEADS), rows_s[3].reshape(1, dbsz, lq, *hd),
             rows_s[4].reshape(1, dbsz, lq, *hd))
    return (y_prompt, y_sample) + out_p + out_s
```

```python
import functools

import jax
import jax.numpy as jnp
from jax import lax
from jax.experimental import pallas as pl
from jax.experimental.pallas import tpu as pltpu

F32 = jnp.float32
BF16 = jnp.bfloat16

D_MODEL = 1024
HEAD_DIM = 64
N_HEADS = 8
WIDTH = N_HEADS * HEAD_DIM
N_PAIRS = N_HEADS // 2
NORM_EPS = 1e-6
MASK_VALUE = -1e30
ATTN_SCALE = HEAD_DIM ** -0.5

PEER_HEADS = 8
PEER_N_KEYS = 128
PEER_HALF = 128
PEER_TOPK = 16
PEER_PICKS = PEER_HEADS * PEER_TOPK

LANES = 128
SUBLANES = 8
ROWS_PER_EXPERT = D_MODEL // 2 // LANES

TOKEN_BLOCK = 256
ATTN_BLOCK = 256
PAST_CHUNK_FOX = 512
PAST_CHUNK_SB = 128
PEER_TOKEN_BLOCK = 128
ACT_RING = (4, 8)
OUT_RING = (4, 4)
VMEM_LIMIT = 56 * 1024 * 1024


def _cparams(n_grid, vmem=None):
    return pltpu.CompilerParams(
        dimension_semantics=("arbitrary",) * n_grid,
        vmem_limit_bytes=vmem,
    )


def _dot(a, b):
    return jnp.dot(a, b, preferred_element_type=F32)


def _dot_nt(a, b):
    return lax.dot_general(a, b, (((1,), (1,)), ((), ())), preferred_element_type=F32)


def _split_dot(a, b_bf16, terms):
    out = None
    rem = a
    for i in range(terms):
        part = rem.astype(BF16)
        d = _dot(part, b_bf16)
        out = d if out is None else out + d
        if i + 1 < terms:
            rem = rem - part.astype(F32)
    return out


def _log_sigmoid(x):
    return jnp.minimum(x, 0.0) - jnp.log(1.0 + jnp.exp(-jnp.abs(x)))


def _sigmoid(x):
    return 1.0 / (1.0 + jnp.exp(-x))


def _in_proj_kernel(x_ref, g_ref, wa_ref, wf_ref, wb_ref, wg_ref, bf_ref, qn_ref, kn_ref, bd_ref,
                    fq_ref, fk_ref, fv_ref, lf_ref, sq_ref, sk_ref, sv_ref, ga_ref, gb_ref):
    x = x_ref[...]
    ms = jnp.mean(x * x, axis=-1, keepdims=True)
    n = (x * lax.rsqrt(ms + NORM_EPS) * g_ref[...]).astype(BF16)

    a = _dot(n, wa_ref[...])
    bd = bd_ref[...]

    def head_norm(t, gain):
        msq = _split_dot(t * t, bd, 2) * (1.0 / HEAD_DIM)
        return t * lax.rsqrt(msq + NORM_EPS) * gain

    fq_ref[...] = head_norm(a[:, :WIDTH], qn_ref[...])
    fk_ref[...] = head_norm(a[:, WIDTH:2 * WIDTH], kn_ref[...])
    fv_ref[...] = a[:, 2 * WIDTH:]

    f = _dot(n, wf_ref[...]) + bf_ref[...]
    lf_ref[...] = _log_sigmoid(f)[:, :N_HEADS]

    b = _dot(n, wb_ref[...])
    sq_ref[...] = b[:, :WIDTH]
    sk_ref[...] = b[:, WIDTH:2 * WIDTH]
    sv_ref[...] = b[:, 2 * WIDTH:]

    g = _dot(n, wg_ref[...])
    ga_ref[...] = _sigmoid(g[:, :D_MODEL])
    gb_ref[...] = _sigmoid(g[:, D_MODEL:])


def _in_proj(x, p):
    t = x.shape[0]
    tl = min(TOKEN_BLOCK, t)
    assert t % tl == 0
    row = lambda w: pl.BlockSpec((tl, w), lambda i: (i, 0))
    full = lambda a: pl.BlockSpec(a.shape, lambda i: (0,) * a.ndim)
    consts = (p["g_mix"], p["w_a"], p["w_f"], p["w_b"], p["w_g"], p["b_f"], p["qn"], p["kn"], p["bd"])
    widths = (WIDTH, WIDTH, WIDTH, N_HEADS, WIDTH, WIDTH, WIDTH, D_MODEL, D_MODEL)
    return pl.pallas_call(
        _in_proj_kernel,
        out_shape=tuple(jax.ShapeDtypeStruct((t, w), F32) for w in widths),
        grid=(t // tl,),
        in_specs=[row(D_MODEL)] + [full(c) for c in consts],
        out_specs=tuple(row(w) for w in widths),
        compiler_params=_cparams(1, VMEM_LIMIT),
        name="in_proj",
    )(x, *consts)


def _cumsum_kernel(x_ref, o_ref):
    n_chunks = x_ref.shape[1] // LANES
    r = lax.broadcasted_iota(jnp.int32, (LANES, LANES), 0)
    c = lax.broadcasted_iota(jnp.int32, (LANES, LANES), 1)
    upper = jnp.where(r <= c, 1.0, 0.0).astype(BF16)
    carry = jnp.zeros((x_ref.shape[0], 1), F32)
    for k in range(n_chunks):
        y = _split_dot(x_ref[:, k * LANES:(k + 1) * LANES], upper, 3) + carry
        o_ref[:, k * LANES:(k + 1) * LANES] = y
        carry = y[:, LANES - 1:LANES]


def _logf_cumsum(lf_rows):
    return pl.pallas_call(
        _cumsum_kernel,
        out_shape=jax.ShapeDtypeStruct(lf_rows.shape, F32),
        name="logf_cumsum",
    )(lf_rows)


def _head_masks():
    lane = lax.broadcasted_iota(jnp.int32, (1, LANES), 1)
    return lane < HEAD_DIM, lane >= HEAD_DIM


def _stack_heads(x):
    m0, m1 = _head_masks()
    return jnp.concatenate([jnp.where(m0, x, 0.0), jnp.where(m1, x, 0.0)], axis=0)


def _unstack_heads(y, tq):
    m0, _ = _head_masks()
    return jnp.where(m0, y[:tq], y[tq:])


def _stacked_pos(tq, tk):
    r = lax.broadcasted_iota(jnp.int32, (2 * tq, tk), 0)
    c = lax.broadcasted_iota(jnp.int32, (2 * tq, tk), 1)
    return jnp.where(r >= tq, r - tq, r), c


def _static_chunks(n, size):
    out = [(s, size) for s in range(0, n - size + 1, size)]
    done = len(out) * size
    if done < n:
        out.append((done, n - done))
    return out


def _tri_strict(n):
    r = lax.broadcasted_iota(jnp.int32, (n, n), 0)
    c = lax.broadcasted_iota(jnp.int32, (n, n), 1)
    return jnp.where(r > c, 1.0, 0.0).astype(BF16)


def _softmax_step(carry, qs, is_head1, segments):
    scores = []
    for kb, _, ck, valid in segments:
        bias = jnp.where(is_head1, ck[1:2, :], ck[0:1, :])
        s = _dot_nt(qs, kb.astype(BF16)) - bias
        if valid is not None:
            s = jnp.where(valid, s, MASK_VALUE)
        scores.append(s)
    m_new = functools.reduce(jnp.maximum, [jnp.max(s, axis=-1, keepdims=True) for s in scores])
    if carry is not None:
        m_old, l_old, acc_old = carry
        m_new = jnp.maximum(m_old, m_new)
    l = None
    acc = None
    for s, (_, vb, _, _) in zip(scores, segments):
        pr = jnp.exp(s - m_new)
        ls = jnp.sum(pr, axis=-1, keepdims=True)
        pv = _dot(pr.astype(BF16), vb.astype(BF16))
        l = ls if l is None else l + ls
        acc = pv if acc is None else acc + pv
    if carry is not None:
        alpha = jnp.exp(m_old - m_new)
        l = alpha * l_old + l
        acc = alpha * acc_old + acc
    return m_new, l, acc


def _fox_kernel(q_ref, kn_ref, vn_ref, kp_ref, vp_ref, cn_ref, cp_ref, o_ref, *, tq, p_len):
    lq = q_ref.shape[1]
    nq = lq // tq
    row, col = _stacked_pos(tq, tq)
    causal = col <= row
    is_head1 = lax.broadcasted_iota(jnp.int32, (2 * tq, 1), 0) >= tq
    past_size = p_len if nq > 1 else PAST_CHUNK_FOX
    past = [(kp_ref[0, s:s + n, :], vp_ref[0, s:s + n, :], cp_ref[0, 0, :, s:s + n], None)
            for s, n in _static_chunks(p_len, past_size)]

    def q_block(i):
        q0 = i * tq if nq == 1 else pl.multiple_of(i * tq, tq)
        qs = _stack_heads(q_ref[0, pl.ds(q0, tq), :] * ATTN_SCALE).astype(BF16)
        diag = (kn_ref[0, pl.ds(q0, tq), :], vn_ref[0, pl.ds(q0, tq), :], cn_ref[0, 0, i], causal)

        if nq > 1:
            def seg_of(j):
                k0 = pl.multiple_of(j * tq, tq)
                return (kn_ref[0, pl.ds(k0, tq), :], vn_ref[0, pl.ds(k0, tq), :], cn_ref[0, 0, j], None)

            carry = lax.cond(i % 2 == 1,
                             lambda: _softmax_step(None, qs, is_head1, past + [diag, seg_of(i - 1)]),
                             lambda: _softmax_step(None, qs, is_head1, past + [diag]))

            def pair_body(jj, carry):
                return _softmax_step(carry, qs, is_head1, [seg_of(2 * jj), seg_of(2 * jj + 1)])

            carry = lax.fori_loop(0, i // 2, pair_body, carry)
        else:
            carry = _softmax_step(None, qs, is_head1, past + [diag])
        _, l, acc = carry
        o_ref[0, pl.ds(q0, tq), :] = _unstack_heads(acc / l, tq)

    if nq == 1:
        q_block(0)
    else:
        lax.fori_loop(0, nq, lambda i, _: (q_block(i), 0)[1], 0)


def _sb_kernel(q_ref, kn_ref, vn_ref, kp_ref, vp_ref, o_ref, *, tq, p_len):
    lq = q_ref.shape[1]
    nq = lq // tq
    row, col = _stacked_pos(tq, tq)
    strictly_before = col < row
    tri_q = _tri_strict(tq)

    def local(qs, kb, tri_m, valid=None):
        z = _dot_nt(qs, kb.astype(BF16))
        m = _log_sigmoid(-z)
        if valid is not None:
            m = jnp.where(valid, m, 0.0)
        tail_in = _split_dot(m, tri_m, 2)
        return z, m, tail_in, tail_in[:, 0:1] + m[:, 0:1]

    def weights(z, m, tail_in, run, valid=None):
        a = jnp.exp(z + m + tail_in) if run is None else jnp.exp(z + m + tail_in + run)
        if valid is not None:
            a = jnp.where(valid, a, 0.0)
        return a.astype(BF16)

    def q_block(i):
        q0 = i * tq if nq == 1 else pl.multiple_of(i * tq, tq)
        qs = _stack_heads(q_ref[0, pl.ds(q0, tq), :] * ATTN_SCALE).astype(BF16)
        def diag_step():
            z, m, tail_in, tot = local(qs, kn_ref[0, pl.ds(q0, tq), :], tri_q, strictly_before)
            acc = _dot(weights(z, m, tail_in, None, strictly_before), vn_ref[0, pl.ds(q0, tq), :].astype(BF16))
            return tot, acc

        if nq > 1:
            assert p_len <= ATTN_BLOCK

            def blk(j):
                k0 = pl.multiple_of(j * tq, tq)
                return kn_ref[0, pl.ds(k0, tq), :], vn_ref[0, pl.ds(k0, tq), :].astype(BF16)

            def first_step(with_block):
                zp, mp, tp, _ = local(qs, kp_ref[0], _tri_strict(p_len))
                past_acc = _dot(weights(zp, mp, tp, None), vp_ref[0].astype(BF16))
                run, acc = diag_step()
                if with_block:
                    k1, v1 = blk(i - 1)
                    z1, m1, t1, tot1 = local(qs, k1, tri_q)
                    acc = acc + _dot(weights(z1, m1, t1, run), v1)
                    run = run + tot1
                return run, acc, past_acc

            odd = i % 2
            run, acc, past_acc = lax.cond(odd == 1, lambda: first_step(True), lambda: first_step(False))

            def pair_body(jj, carry):
                run, acc = carry
                k1, v1 = blk(i - odd - 1 - 2 * jj)
                k2, v2 = blk(i - odd - 2 - 2 * jj)
                z1, m1, t1, tot1 = local(qs, k1, tri_q)
                z2, m2, t2, tot2 = local(qs, k2, tri_q)
                acc = acc + _dot(weights(z1, m1, t1, run), v1) + _dot(weights(z2, m2, t2, run + tot1), v2)
                return run + (tot1 + tot2), acc

            run, acc = lax.fori_loop(0, i // 2, pair_body, (run, acc))
            acc = acc + jnp.exp(run) * past_acc
        else:
            run, acc = diag_step()
            chunks = _static_chunks(p_len, PAST_CHUNK_SB)
            tris = {n: _tri_strict(n) for n in {n for _, n in chunks}}
            for s, n in reversed(chunks):
                z, m, tail_in, tot = local(qs, kp_ref[0, s:s + n, :], tris[n])
                acc = acc + _dot(weights(z, m, tail_in, run), vp_ref[0, s:s + n, :].astype(BF16))
                run = run + tot
        o_ref[0, pl.ds(q0, tq), :] = _unstack_heads(acc, tq)

    if nq == 1:
        q_block(0)
    else:
        lax.fori_loop(0, nq, lambda i, _: (q_block(i), 0)[1], 0)


def _attn_specs(bsz, lq, p_len, past_shared):
    pair_new = pl.BlockSpec((1, lq, LANES), lambda b, p: (b, 0, p))
    if past_shared:
        pair_past = pl.BlockSpec((1, p_len, LANES), lambda b, p: (0, 0, p))
    else:
        pair_past = pl.BlockSpec((1, p_len, LANES), lambda b, p: (b, 0, p))
    return pair_new, pair_past


def _fox_attn(q, k_new, v_new, k_past, v_past, cn, cp):
    bsz, lq, _ = q.shape
    p_len = k_past.shape[1]
    tq = min(ATTN_BLOCK, lq)
    pair_new, pair_past = _attn_specs(bsz, lq, p_len, k_past.shape[0] == 1)
    c_spec = lambda a: pl.BlockSpec((1, 1) + a.shape[2:], lambda b, p: (b, p) + (0,) * (a.ndim - 2))
    return pl.pallas_call(
        functools.partial(_fox_kernel, tq=tq, p_len=p_len),
        out_shape=jax.ShapeDtypeStruct(q.shape, F32),
        grid=(bsz, N_PAIRS),
        in_specs=[pair_new, pair_new, pair_new, pair_past, pair_past, c_spec(cn), c_spec(cp)],
        out_specs=pair_new,
        compiler_params=_cparams(2, VMEM_LIMIT),
        name="fox_attn",
    )(q, k_new, v_new, k_past, v_past, cn, cp)


def _sb_attn(q, k_new, v_new, k_past, v_past):
    bsz, lq, _ = q.shape
    p_len = k_past.shape[1]
    tq = min(ATTN_BLOCK, lq)
    pair_new, pair_past = _attn_specs(bsz, lq, p_len, k_past.shape[0] == 1)
    return pl.pallas_call(
        functools.partial(_sb_kernel, tq=tq, p_len=p_len),
        out_shape=jax.ShapeDtypeStruct(q.shape, F32),
        grid=(bsz, N_PAIRS),
        in_specs=[pair_new, pair_new, pair_new, pair_past, pair_past],
        out_specs=pair_new,
        compiler_params=_cparams(2, VMEM_LIMIT),
        name="sb_attn",
    )(q, k_new, v_new, k_past, v_past)


def _topk_rows(vals, payload, k):
    n = vals.shape[0]
    rio = lax.broadcasted_iota(jnp.int32, vals.shape, 0)
    out_v, out_p = [], []
    for _ in range(k):
        mx = jnp.max(vals, axis=0, keepdims=True)
        pos = jnp.min(jnp.where(vals == mx, rio, n), axis=0, keepdims=True)
        sel = rio == pos
        out_v.append(mx)
        if payload is None:
            out_p.append(pos)
        else:
            out_p.append(jnp.sum(jnp.where(sel, payload, 0), axis=0, keepdims=True))
        vals = jnp.where(sel, -jnp.inf, vals)
    return jnp.concatenate(out_v, axis=0), jnp.concatenate(out_p, axis=0)


def _pair_rows(first, second):
    half = PEER_TOPK // 2
    rows = [first[0:1, :] + second]
    rows += [first[a:a + 1, :] + second[:half] for a in range(1, half)]
    rows.append(first[half:, :] + second[0:1, :])
    return jnp.concatenate(rows, axis=0)


def _mix_route_kernel(x_ref, of_ref, os_ref, ga_ref, gb_ref, wof_ref, wos_ref, wout_ref, gf_ref,
                      wq_ref, k1_ref, k2_ref, h_ref, n2_ref, idx_ref, gate_ref):
    yf = _dot(of_ref[...].astype(BF16), wof_ref[...])
    ys = _dot(os_ref[...].astype(BF16), wos_ref[...])
    merged = ga_ref[...] * yf + gb_ref[...] * ys
    h = x_ref[...] + _dot(merged.astype(BF16), wout_ref[...])
    h_ref[...] = h
    ms = jnp.mean(h * h, axis=-1, keepdims=True)
    n2 = h * lax.rsqrt(ms + NORM_EPS) * gf_ref[...]
    n2_ref[...] = n2
    qp = _dot(n2.astype(BF16), wq_ref[...]).astype(BF16)
    k1 = k1_ref[...]
    k2 = k2_ref[...]
    tl = x_ref.shape[0]
    for c0 in range(0, tl, LANES):
        rows = []
        for hd in range(PEER_HEADS):
            base = hd * 2 * PEER_HALF
            q1 = qp[c0:c0 + LANES, base:base + PEER_HALF]
            q2 = qp[c0:c0 + LANES, base + PEER_HALF:base + 2 * PEER_HALF]
            s1 = _dot_nt(k1, q1)
            s2 = _dot_nt(k2, q2)
            v1, i1 = _topk_rows(s1, None, PEER_TOPK)
            v2, i2 = _topk_rows(s2, None, PEER_TOPK)
            sc, e = _topk_rows(_pair_rows(v1, v2), _pair_rows(i1 * PEER_N_KEYS, i2), PEER_TOPK)
            ex = jnp.exp(sc - jnp.max(sc, axis=0, keepdims=True))
            gate = ex / jnp.sum(ex, axis=0, keepdims=True)
            r0 = hd * PEER_TOPK
            rows.append(e * ROWS_PER_EXPERT)
            gate_ref[r0:r0 + PEER_TOPK, c0:c0 + LANES] = gate
        idx_ref[c0:c0 + LANES, :] = jnp.concatenate(rows, axis=0).T


def _mix_route(x, o_fox, o_sb, ga, gb, p):
    t = x.shape[0]
    tl = min(TOKEN_BLOCK, t)
    assert t % tl == 0 and tl % LANES == 0
    row = lambda w: pl.BlockSpec((tl, w), lambda i: (i, 0))
    full = lambda a: pl.BlockSpec(a.shape, lambda i: (0,) * a.ndim)
    colblk = pl.BlockSpec((PEER_PICKS, tl), lambda i: (0, i))
    consts_a = (p["w_o_fox"], p["w_o_sb"], p["w_out"], p["g_ffn"], p["w_peer_q"], p["keys1"], p["keys2"])
    return pl.pallas_call(
        _mix_route_kernel,
        out_shape=(jax.ShapeDtypeStruct((t, D_MODEL), F32), jax.ShapeDtypeStruct((t, D_MODEL), F32),
                   jax.ShapeDtypeStruct((t, PEER_PICKS), jnp.int32), jax.ShapeDtypeStruct((PEER_PICKS, t), F32)),
        grid=(t // tl,),
        in_specs=[row(D_MODEL), row(WIDTH), row(WIDTH), row(D_MODEL), row(D_MODEL)] + [full(c) for c in consts_a],
        out_specs=(row(D_MODEL), row(D_MODEL), row(PEER_PICKS), colblk),
        compiler_params=_cparams(1, VMEM_LIMIT),
        name="mix_route",
    )(x, o_fox, o_sb, ga, gb, *consts_a)


def _unpack_pair(words):
    lo = pltpu.bitcast(words << 16, F32)
    hi = pltpu.bitcast(words & jnp.int32(-65536), F32)
    return lo, hi


def _route_copy(src_hbm, slots, sems, chunk, slot):
    rows = slots.shape[1]
    row0 = pl.multiple_of(chunk * rows, rows)
    return pltpu.make_async_copy(src_hbm.at[pl.ds(row0, rows)], slots.at[slot], sems.at[slot])


def _for_each_token(route_hbm, slots, sems, token_fn, carry):
    n_slots, rows, _ = slots.shape
    step = pl.program_id(0)
    chunks_per_step = PEER_TOKEN_BLOCK // rows
    total_chunks = pl.num_programs(0) * chunks_per_step
    ahead = n_slots - 1

    @pl.when(step == 0)
    def _():
        for c in range(ahead):
            _route_copy(route_hbm, slots, sems, c, c).start()

    def slot_round(g, carry):
        for slot in range(n_slots):
            local = g * n_slots + slot
            chunk = step * chunks_per_step + local

            @pl.when(chunk + ahead < total_chunks)
            def _():
                _route_copy(route_hbm, slots, sems, chunk + ahead, (slot + ahead) % n_slots).start()

            _route_copy(route_hbm, slots, sems, chunk, slot).wait()
            for u in range(rows):
                carry = token_fn(local * rows + u, slot, u, carry)
        return carry

    return lax.fori_loop(0, chunks_per_step // n_slots, slot_round, carry)


def _gather_pair(tbl_ref, row_a, row_b):
    ra = pl.multiple_of(row_a, ROWS_PER_EXPERT)
    rb = pl.multiple_of(row_b, ROWS_PER_EXPERT)
    words = jnp.concatenate([tbl_ref[pl.ds(ra, ROWS_PER_EXPERT), :], tbl_ref[pl.ds(rb, ROWS_PER_EXPERT), :]], axis=0)
    return _unpack_pair(words)


def _peer_act_kernel(idx_hbm, idx_ref, x_ref, gate_ref, tbl_ref, pk_ref, slots, sems, prod_a, prod_b):
    tb = x_ref.shape[0]
    lane = lax.broadcasted_iota(jnp.int32, (PEER_PICKS, tb), 1)
    prods = (prod_a, prod_b)
    prod_b[...] = jnp.zeros_like(prod_b)

    def reduce(t, prod_ref, dots):
        part = prod_ref[pl.ds(0, PEER_PICKS, stride=ROWS_PER_EXPERT), :]
        for r in range(1, ROWS_PER_EXPERT):
            part = part + prod_ref[pl.ds(r, PEER_PICKS, stride=ROWS_PER_EXPERT), :]
        d = jnp.sum(part, axis=-1, keepdims=True)
        return jnp.where(lane == t, d, dots)

    def token(t, slot, u, dots):
        xt = x_ref[t]
        x_lo = jnp.concatenate([xt[:ROWS_PER_EXPERT]] * 2, axis=0)
        x_hi = jnp.concatenate([xt[ROWS_PER_EXPERT:]] * 2, axis=0)
        prod_ref = prods[u % 2]
        for j in range(0, PEER_PICKS, 2):
            lo, hi = _gather_pair(tbl_ref, slots[slot, u, j], slots[slot, u, j + 1])
            prod_ref[j * ROWS_PER_EXPERT:(j + 2) * ROWS_PER_EXPERT, :] = lo * x_lo + hi * x_hi
        return reduce(t - 1, prods[1 - u % 2], dots)

    dots = _for_each_token(idx_hbm, slots, sems, token, jnp.zeros((PEER_PICKS, tb), F32))
    dots = reduce(tb - 1, prods[(tb - 1) % 2], dots)
    act = 0.5 * dots * (1.0 + lax.erf(dots * (2.0 ** -0.5)))
    w = (gate_ref[...] * act).T.astype(BF16).astype(F32)
    pk_ref[...] = pltpu.bitcast(w, jnp.int32) | idx_ref[...]


def _peer_out_kernel(pk_hbm, h_ref, tbl_ref, o_ref, slots, sems):
    n_acc = 2
    upper = lax.broadcasted_iota(jnp.int32, (SUBLANES, LANES), 0) >= ROWS_PER_EXPERT

    def token(t, slot, u, carry):
        acc_lo = [jnp.zeros((SUBLANES, LANES), F32) for _ in range(n_acc)]
        acc_hi = [jnp.zeros((SUBLANES, LANES), F32) for _ in range(n_acc)]
        for j in range(0, PEER_PICKS, 2):
            wa = slots[slot, u, j]
            wb = slots[slot, u, j + 1]
            lo, hi = _gather_pair(tbl_ref, wa & 0xFFFF, wb & 0xFFFF)
            wv = pltpu.bitcast(jnp.where(upper, wb, wa) & jnp.int32(-65536), F32)
            k = (j // 2) % n_acc
            acc_lo[k] = acc_lo[k] + wv * lo
            acc_hi[k] = acc_hi[k] + wv * hi
        lo = acc_lo[0] + acc_lo[1]
        hi = acc_hi[0] + acc_hi[1]
        peer = jnp.concatenate([lo[:ROWS_PER_EXPERT] + lo[ROWS_PER_EXPERT:],
                                hi[:ROWS_PER_EXPERT] + hi[ROWS_PER_EXPERT:]], axis=0)
        o_ref[t] = h_ref[t] + peer
        return carry

    _for_each_token(pk_hbm, slots, sems, token, 0)


def _peer_specs(t, ring):
    tb = PEER_TOKEN_BLOCK
    n_slots, rows = ring
    assert t % tb == 0 and tb % (rows * n_slots) == 0 and rows % 2 == 0
    hbm = pl.BlockSpec(memory_space=pl.ANY)
    vcol = pl.BlockSpec((PEER_PICKS, tb), lambda i: (0, i))
    vrow = pl.BlockSpec((tb, PEER_PICKS), lambda i: (i, 0))
    tok = pl.BlockSpec((tb, SUBLANES, LANES), lambda i: (i, 0, 0))
    route_scratch = [pltpu.SMEM((n_slots, rows, PEER_PICKS), jnp.int32), pltpu.SemaphoreType.DMA((n_slots,))]
    return tb, hbm, vcol, vrow, tok, route_scratch


def _table_spec(tbl):
    return pl.BlockSpec(tbl.shape, lambda i: (0, 0), pipeline_mode=pl.Buffered(1))


def _peer_act(idx, n2_r, gate_t, tbl_u):
    t = n2_r.shape[0]
    tb, hbm, vcol, vrow, tok, route_scratch = _peer_specs(t, ACT_RING)
    prod = pltpu.VMEM((PEER_PICKS * ROWS_PER_EXPERT, LANES), F32)
    return pl.pallas_call(
        _peer_act_kernel,
        out_shape=jax.ShapeDtypeStruct((t, PEER_PICKS), jnp.int32),
        grid=(t // tb,),
        in_specs=[hbm, vrow, tok, vcol, _table_spec(tbl_u)],
        out_specs=vrow,
        scratch_shapes=route_scratch + [prod, prod],
        compiler_params=_cparams(1, VMEM_LIMIT),
        name="peer_act",
    )(idx, idx, n2_r, gate_t, tbl_u)


def _peer_out(pk, h_r, tbl_v):
    t = h_r.shape[0]
    tb, hbm, vcol, vrow, tok, route_scratch = _peer_specs(t, OUT_RING)
    return pl.pallas_call(
        _peer_out_kernel,
        out_shape=jax.ShapeDtypeStruct(h_r.shape, F32),
        grid=(t // tb,),
        in_specs=[hbm, tok, _table_spec(tbl_v)],
        out_specs=tok,
        scratch_shapes=route_scratch,
        compiler_params=_cparams(1, VMEM_LIMIT),
        name="peer_out",
    )(pk, h_r, tbl_v)


def _pack_table_kernel(t_ref, o_ref):
    half = D_MODEL // 2
    as_bits = lambda x: pltpu.bitcast(x.astype(BF16).astype(F32), jnp.int32)
    lo = lax.shift_right_logical(as_bits(t_ref[:, :half]), 16)
    words = as_bits(t_ref[:, half:]) | lo
    n = t_ref.shape[0]
    for r in range(ROWS_PER_EXPERT):
        o_ref[pl.ds(r, n, stride=ROWS_PER_EXPERT), :] = words[:, r * LANES:(r + 1) * LANES]


def _pack_table(tbl):
    n_exp = tbl.shape[0]
    blk = 512
    assert n_exp % blk == 0
    return pl.pallas_call(
        _pack_table_kernel,
        out_shape=jax.ShapeDtypeStruct((n_exp * ROWS_PER_EXPERT, LANES), jnp.int32),
        grid=(n_exp // blk,),
        in_specs=[pl.BlockSpec((blk, D_MODEL), lambda i: (i, 0))],
        out_specs=pl.BlockSpec((blk * ROWS_PER_EXPERT, LANES), lambda i: (i, 0)),
        compiler_params=_cparams(1),
        name="pack_table",
    )(tbl)


def _prep_params(norm_mix, w_in, b_forget, fox_q_norm, fox_k_norm, w_o_fox, w_o_sb, w_out,
                 norm_ffn, w_peer_q, keys1, keys2, expert_u, expert_v):
    w = w_in.astype(BF16)
    o_f = 3 * WIDTH
    o_b = o_f + N_HEADS
    o_g = o_b + 3 * WIDTH
    pad = LANES - N_HEADS
    head = lax.broadcasted_iota(jnp.int32, (WIDTH, WIDTH), 0) // HEAD_DIM
    head_t = lax.broadcasted_iota(jnp.int32, (WIDTH, WIDTH), 1) // HEAD_DIM
    return {
        "g_mix": norm_mix.reshape(1, D_MODEL),
        "w_a": w[:, :o_f],
        "w_f": jnp.pad(w[:, o_f:o_b], ((0, 0), (0, pad))),
        "w_b": w[:, o_b:o_g],
        "w_g": w[:, o_g:],
        "b_f": jnp.pad(b_forget.reshape(1, N_HEADS), ((0, 0), (0, pad))),
        "qn": jnp.tile(fox_q_norm.reshape(1, HEAD_DIM), (1, N_HEADS)),
        "kn": jnp.tile(fox_k_norm.reshape(1, HEAD_DIM), (1, N_HEADS)),
        "bd": (head == head_t).astype(BF16),
        "w_o_fox": w_o_fox.astype(BF16),
        "w_o_sb": w_o_sb.astype(BF16),
        "w_out": w_out.astype(BF16),
        "g_ffn": norm_ffn.reshape(1, D_MODEL),
        "w_peer_q": w_peer_q.astype(BF16),
        "keys1": keys1.astype(BF16),
        "keys2": keys2.astype(BF16),
        "tbl_u": _pack_table(expert_u),
        "tbl_v": _pack_table(expert_v),
    }


def _forget_bias_layouts(lf_past, lf_new, tq):
    bsz, lq, _ = lf_new.shape
    p_len = lf_past.shape[1]
    total = p_len + lq
    lp = -(-total // LANES) * LANES
    lf_all = jnp.concatenate([jnp.broadcast_to(lf_past, (bsz, p_len, N_HEADS)), lf_new], axis=1)
    rows = jnp.pad(jnp.transpose(lf_all, (0, 2, 1)), ((0, 0), (0, 0), (0, lp - total)))
    c = _logf_cumsum(rows.reshape(bsz * N_HEADS, lp)).reshape(bsz, N_PAIRS, 2, lp)
    c_new = c[..., p_len:total]
    cn =jnp.transpose(c_new.reshape(bsz, N_PAIRS, 2, lq // tq, tq), (0, 1, 3, 2, 4))
    cp = c[..., :p_len]
    return cn, cp


def _group_forward(x, past, p):
    bsz, lq, _ = x.shape
    t = bsz * lq
    x2 = x.reshape(t, D_MODEL)
    fq, fk, fv, lf, sq, sk, sv, ga, gb = _in_proj(x2, p)
    b3 = lambda a: a.reshape(bsz, lq, a.shape[-1])
    pk, pv, plf, psk, psv = past
    tq = min(ATTN_BLOCK, lq)
    cn, cp = _forget_bias_layouts(plf, b3(lf), tq)
    o_fox = _fox_attn(b3(fq), b3(fk), b3(fv), pk, pv, cn, cp)
    o_sb = _sb_attn(b3(sq), b3(sk), b3(sv), psk, psv)
    h, n2, idx, gate_t = _mix_route(x2, o_fox.reshape(t, WIDTH), o_sb.reshape(t, WIDTH), ga, gb, p)
    tile = lambda a: a.reshape(t, SUBLANES, LANES)
    routes = _peer_act(idx, tile(n2), gate_t, p["tbl_u"])
    out = _peer_out(routes, tile(h), p["tbl_v"])
    return out.reshape(bsz, lq, D_MODEL), (b3(fk), b3(fv), b3(lf), b3(sk), b3(sv))


def kernel(x_prompt, x_sample, cache_fox_k, cache_fox_v, cache_fox_logf, cache_sb_k, cache_sb_v,
           meta_tokens, norm_mix, w_in, b_forget, fox_q_norm, fox_k_norm, w_o_fox, w_o_sb, w_out,
           norm_ffn, w_peer_q, peer_sub_keys_1, peer_sub_keys_2, expert_u, expert_v):
    assert norm_mix.shape[0] == 1, "single-layer stack"
    p = _prep_params(norm_mix[0], w_in[0], b_forget[0], fox_q_norm[0], fox_k_norm[0], w_o_fox[0],
                     w_o_sb[0], w_out[0], norm_ffn[0], w_peer_q[0], peer_sub_keys_1[0],
                     peer_sub_keys_2[0], expert_u[0], expert_v[0])
    bsz = x_prompt.shape[0]
    n_meta = meta_tokens.shape[0]

    _, mk, mv, mlf, _, msk, msv, _, _ = _in_proj(meta_tokens.astype(x_prompt.dtype), p)
    meta_past = (mk[None], mv[None], mlf[None], msk[None], msv[None])
    y_prompt, rows_p = _group_forward(x_prompt, meta_past, p)

    def with_meta(meta_rows, new_rows, tail_shape):
        full = jnp.concatenate([jnp.broadcast_to(meta_rows[None], (bsz,) + meta_rows.shape), new_rows], axis=1)
        return full.reshape((1, bsz, n_meta + new_rows.shape[1]) + tail_shape)

    hd = (N_HEADS, HEAD_DIM)
    out_p = (with_meta(mk, rows_p[0], hd), with_meta(mv, rows_p[1], hd), with_meta(mlf, rows_p[2], (N_HEADS,)),
             with_meta(msk, rows_p[3], hd), with_meta(msv, rows_p[4], hd))

    dbsz, plen = cache_fox_k.shape[1], cache_fox_k.shape[2]
    flat = lambda c: c[0].reshape(dbsz, plen, -1)
    sample_past = (flat(cache_fox_k), flat(cache_fox_v), flat(cache_fox_logf), flat(cache_sb_k), flat(cache_sb_v))
    y_sample, rows_s = _group_forward(x_sample, sample_past, p)
    lq = x_sample.shape[1]
    out_s = (rows_s[0].reshape(1, dbsz, lq, *hd), rows_s[1].reshape(1, dbsz, lq, *hd),
             rows_s[2].reshape(1, dbsz, lq, N_HEADS), rows_s[3].reshape(1, dbsz, lq, *hd),
             rows_s[4].reshape(1, dbsz, lq, *hd))
    return (y_prompt, y_sample) + out_p + out_s
```
